```python
import math
import jax, jax.numpy as jnp
from jax import lax
import numpy as np

D_MODEL = 2048
BATCH = 4
SEQ = 4096
DEPTH = 2

DIFF_HEADS = 8
DIFF_HEAD_DIM = 64
DIFF_QK_WIDTH = DIFF_HEADS * 2 * DIFF_HEAD_DIM
DIFF_V_WIDTH = DIFF_HEADS * 2 * DIFF_HEAD_DIM
DIFF_Q_BLOCK = 128
MOBA_HEADS = 8
MOBA_HEAD_DIM = 128
MOBA_WIDTH = MOBA_HEADS * MOBA_HEAD_DIM
MOBA_BLOCK = 256
MOBA_TOPK = 3
MOBA_Q_CHUNK = 32
ROPE_THETA = 500000.0
ROT_FRAC_DIV = 4
D_FF = 5632
N_BRANCHES = 2
IN_WIDTH = 2 * DIFF_QK_WIDTH + DIFF_V_WIDTH + 3 * MOBA_WIDTH + N_BRANCHES * D_MODEL
NORM_EPS = 1e-6
SUBLN_EPS = 1e-5
NEG_INF = -1e30

kernel_name = "hybrid_diffattn_moba_macaron"


def rms_norm(x, g, eps=NORM_EPS):
    xf = x.astype(jnp.float32)
    y = xf * lax.rsqrt(jnp.mean(xf * xf, axis=-1, keepdims=True) + eps)
    return (y * g.astype(jnp.float32)).astype(x.dtype)


def rope_tables(seq, head_dim):
    rot = head_dim // ROT_FRAC_DIV
    inv = jnp.power(ROPE_THETA, -jnp.arange(0, rot, 2, dtype=jnp.float32) / rot)
    ang = jnp.arange(seq, dtype=jnp.float32)[:, None] * inv[None, :]
    return jnp.cos(ang), jnp.sin(ang)


def apply_partial_rope(x, cos, sin):
    half = cos.shape[-1]
    rot = 2 * half
    xr, xp = x[..., :rot], x[..., rot:]
    x1, x2 = xr[..., :half], xr[..., half:]
    shape = (1, cos.shape[0]) + (1,) * (x.ndim - 3) + (half,)
    c = cos.reshape(shape).astype(x.dtype)
    s = sin.reshape(shape).astype(x.dtype)
    return jnp.concatenate([x1 * c - x2 * s, x2 * c + x1 * s, xp], axis=-1)


def swiglu(h, w_gu, w_down):
    gu = h @ w_gu
    g, u = gu[..., :D_FF], gu[..., D_FF:]
    return (jax.nn.silu(g) * u) @ w_down


def diff_attention(q, k, v, lq1, lk1, lq2, lk2, subln_g, lambda_init, cos, sin):
    B, S = q.shape[0], q.shape[1]
    q = apply_partial_rope(q, cos, sin)
    k = apply_partial_rope(k, cos, sin)
    f32 = jnp.float32
    lam = (jnp.exp(jnp.sum(lq1.astype(f32) * lk1.astype(f32)))
           - jnp.exp(jnp.sum(lq2.astype(f32) * lk2.astype(f32))) + lambda_init)
    scale = DIFF_HEAD_DIM ** -0.5
    nq = S // DIFF_Q_BLOCK
    qb = q.reshape(B, nq, DIFF_Q_BLOCK, DIFF_HEADS, 2, DIFF_HEAD_DIM).transpose(1, 0, 2, 3, 4, 5)
    starts = jnp.arange(nq, dtype=jnp.int32) * DIFF_Q_BLOCK
    k_pos = jnp.arange(S, dtype=jnp.int32)

    def block(args):
        q_blk, start = args
        s = jnp.einsum('bqhcd,bkhcd->bhcqk', q_blk, k).astype(f32) * scale
        q_pos = start + jnp.arange(DIFF_Q_BLOCK, dtype=jnp.int32)
        s = jnp.where(k_pos[None, :] <= q_pos[:, None], s, NEG_INF)
        p = jax.nn.softmax(s, axis=-1)
        a = p[:, :, 0] - lam * p[:, :, 1]
        return jnp.einsum('bhqk,bkhe->bqhe', a.astype(v.dtype), v)

    o = lax.map(block, (qb, starts))
    o = o.transpose(1, 0, 2, 3, 4).reshape(B, S, DIFF_HEADS, 2 * DIFF_HEAD_DIM)
    o = rms_norm(o, subln_g, SUBLN_EPS) * (1.0 - lambda_init)
    return o.reshape(B, S, DIFF_V_WIDTH)


def moba_attention(q, k, v, cos, sin):
    B, S, H, hd = q.shape
    f32 = jnp.float32
    q = apply_partial_rope(q, cos, sin).transpose(0, 2, 1, 3)
    k = apply_partial_rope(k, cos, sin).transpose(0, 2, 1, 3)
    v = v.transpose(0, 2, 1, 3)
    nb = -(-S // MOBA_BLOCK)
    s_pad = nb * MOBA_BLOCK
    pad = ((0, 0), (0, 0), (0, s_pad - S), (0, 0))
    kp = jnp.pad(k, pad)
    vp = jnp.pad(v, pad)
    kb = kp.reshape(B, H, nb, MOBA_BLOCK, hd)
    vb = vp.reshape(B, H, nb, MOBA_BLOCK, hd)
    k_mean = jnp.mean(kb.astype(f32), axis=3)
    topk = min(MOBA_TOPK, nb)
    scale = hd ** -0.5
    nc = S // MOBA_Q_CHUNK
    qc = q.reshape(B, H, nc, MOBA_Q_CHUNK, hd).transpose(2, 0, 1, 3, 4)
    starts = jnp.arange(nc, dtype=jnp.int32) * MOBA_Q_CHUNK
    blk_ids = jnp.arange(nb, dtype=jnp.int32)
    b_idx = jnp.arange(B)[:, None, None, None]
    h_idx = jnp.arange(H)[None, :, None, None]

    def chunk(args):
        q_c, start = args
        own = start // MOBA_BLOCK
        q_pos = start + jnp.arange(MOBA_Q_CHUNK, dtype=jnp.int32)
        g = jnp.einsum('bhqd,bhnd->bhqn', q_c.astype(f32), k_mean)
        g = jnp.where(blk_ids < own, g, NEG_INF)
        _, sel = lax.top_k(g, topk)
        valid = sel < own
        k_sel = kb[b_idx, h_idx, sel]
        v_sel = vb[b_idx, h_idx, sel]
        s_sel = jnp.einsum('bhqd,bhqrkd->bhqrk', q_c, k_sel).astype(f32) * scale
        s_sel = jnp.where(valid[..., None], s_sel, NEG_INF)
        k_own = lax.dynamic_slice_in_dim(kp, own * MOBA_BLOCK, MOBA_BLOCK, axis=2)
        v_own = lax.dynamic_slice_in_dim(vp, own * MOBA_BLOCK, MOBA_BLOCK, axis=2)
        s_own = jnp.einsum('bhqd,bhkd->bhqk', q_c, k_own).astype(f32) * scale
        own_pos = own * MOBA_BLOCK + jnp.arange(MOBA_BLOCK, dtype=jnp.int32)
        s_own = jnp.where(own_pos[None, :] <= q_pos[:, None], s_own, NEG_INF)
        s_all = jnp.concatenate(
            [s_own, s_sel.reshape(B, H, MOBA_Q_CHUNK, topk * MOBA_BLOCK)], axis=-1)
        p = jax.nn.softmax(s_all, axis=-1).astype(v.dtype)
        p_own = p[..., :MOBA_BLOCK]
        p_sel = p[..., MOBA_BLOCK:].reshape(B, H, MOBA_Q_CHUNK, topk, MOBA_BLOCK)
        return (jnp.einsum('bhqk,bhkd->bhqd', p_own, v_own)
                + jnp.einsum('bhqrk,bhqrkd->bhqd', p_sel, v_sel))

    o = lax.map(chunk, (qc, starts))
    return o.transpose(1, 0, 3, 2, 4).reshape(B, S, H * hd)


def setup_inputs(seed: int = 0) -> dict:
    key = jax.random.key(seed)
    ks = jax.random.split(key, 24)
    f32 = jnp.float32

    def nrm(k, shape, scale):
        return jax.random.normal(k, shape, f32) * scale

    def gain(k, shape):
        return 1.0 + 0.05 * jax.random.normal(k, shape, f32)

    L = DEPTH
    return {
        'x': jax.random.normal(ks[0], (BATCH, SEQ, D_MODEL), f32),
        'ffn1_pre_g': gain(ks[1], (L, D_MODEL)),
        'ffn1_w_gu': nrm(ks[2], (L, D_MODEL, 2 * D_FF), D_MODEL ** -0.5),
        'ffn1_w_down': nrm(ks[3], (L, D_FF, D_MODEL), D_FF ** -0.5),
        'ffn1_post_g': gain(ks[4], (L, D_MODEL)),
        'mix_pre_g': gain(ks[5], (L, D_MODEL)),
        'w_in': nrm(ks[6], (L, D_MODEL, IN_WIDTH), D_MODEL ** -0.5),
        'diff_lq1': nrm(ks[7], (L, DIFF_HEAD_DIM), 0.1),
        'diff_lk1': nrm(ks[8], (L, DIFF_HEAD_DIM), 0.1),
        'diff_lq2': nrm(ks[9], (L, DIFF_HEAD_DIM), 0.1),
        'diff_lk2': nrm(ks[10], (L, DIFF_HEAD_DIM), 0.1),
        'diff_subln_g': gain(ks[11], (L, 2 * DIFF_HEAD_DIM)),
        'w_branch_diff': nrm(ks[12], (L, DIFF_V_WIDTH, D_MODEL), DIFF_V_WIDTH ** -0.5),
        'w_branch_moba': nrm(ks[13], (L, MOBA_WIDTH, D_MODEL), MOBA_WIDTH ** -0.5),
        'w_out': nrm(ks[14], (L, D_MODEL, D_MODEL), D_MODEL ** -0.5),
        'mix_post_g': gain(ks[15], (L, D_MODEL)),
        'ffn2_pre_g': gain(ks[16], (L, D_MODEL)),
        'ffn2_w_gu': nrm(ks[17], (L, D_MODEL, 2 * D_FF), D_MODEL ** -0.5),
        'ffn2_w_down': nrm(ks[18], (L, D_FF, D_MODEL), D_FF ** -0.5),
        'ffn2_post_g': gain(ks[19], (L, D_MODEL)),
    }


def reference(x, ffn1_pre_g, ffn1_w_gu, ffn1_w_down, ffn1_post_g, mix_pre_g, w_in,
              diff_lq1, diff_lk1, diff_lq2, diff_lk2, diff_subln_g,
              w_branch_diff, w_branch_moba, w_out, mix_post_g,
              ffn2_pre_g, ffn2_w_gu, ffn2_w_down, ffn2_post_g):
    B, S, _ = x.shape
    cos_d, sin_d = rope_tables(S, DIFF_HEAD_DIM)
    cos_m, sin_m = rope_tables(S, MOBA_HEAD_DIM)
    o_qa = 0
    o_ka = o_qa + DIFF_QK_WIDTH
    o_va = o_ka + DIFF_QK_WIDTH
    o_qb = o_va + DIFF_V_WIDTH
    o_kb = o_qb + MOBA_WIDTH
    o_vb = o_kb + MOBA_WIDTH
    o_g = o_vb + MOBA_WIDTH
    for l in range(DEPTH):
        lambda_init = 0.8 - 0.6 * math.exp(-0.3 * l)
        h = rms_norm(x, ffn1_pre_g[l])
        x = x + 0.5 * rms_norm(swiglu(h, ffn1_w_gu[l], ffn1_w_down[l]), ffn1_post_g[l])
        h = rms_norm(x, mix_pre_g[l])
        proj = h @ w_in[l]
        qa = proj[..., o_qa:o_ka].reshape(B, S, DIFF_HEADS, 2, DIFF_HEAD_DIM)
        ka = proj[..., o_ka:o_va].reshape(B, S, DIFF_HEADS, 2, DIFF_HEAD_DIM)
        va = proj[..., o_va:o_qb].reshape(B, S, DIFF_HEADS, 2 * DIFF_HEAD_DIM)
        qm = proj[..., o_qb:o_kb].reshape(B, S, MOBA_HEADS, MOBA_HEAD_DIM)
        km = proj[..., o_kb:o_vb].reshape(B, S, MOBA_HEADS, MOBA_HEAD_DIM)
        vm = proj[..., o_vb:o_g].reshape(B, S, MOBA_HEADS, MOBA_HEAD_DIM)
        gate_a = jax.nn.sigmoid(proj[..., o_g:o_g + D_MODEL])
        gate_b = jax.nn.sigmoid(proj[..., o_g + D_MODEL:o_g + 2 * D_MODEL])
        ya = diff_attention(qa, ka, va, diff_lq1[l], diff_lk1[l], diff_lq2[l], diff_lk2[l],
                            diff_subln_g[l], lambda_init, cos_d, sin_d)
        yb = moba_attention(qm, km, vm, cos_m, sin_m)
        merged = gate_a * (ya @ w_branch_diff[l]) + gate_b * (yb @ w_branch_moba[l])
        x = x + rms_norm(merged @ w_out[l], mix_post_g[l])
        h = rms_norm(x, ffn2_pre_g[l])
        x = x + 0.5 * rms_norm(swiglu(h, ffn2_w_gu[l], ffn2_w_down[l]), ffn2_post_g[l])
    return x
```

```python
import functools
import math

import jax
import jax.numpy as jnp
from jax import lax
from jax.experimental import pallas as pl
from jax.experimental.pallas import tpu as pltpu

F32 = jnp.float32
BF16 = jnp.bfloat16

DIFF_HEADS = 8
DIFF_HEAD_DIM = 64
MOBA_HEADS = 8
MOBA_HEAD_DIM = 128
MOBA_BLOCK = 256
MOBA_TOPK = 3
ATTN_WIDTH = 1024
ROPE_THETA = 500000.0
ROT_FRAC_DIV = 4
NORM_EPS = 1e-6
SUBLN_EPS = 1e-5
NEG_INF = -1e30
REMOVED = -3e38

LANES = 128
VMEM_LIMIT_BYTES = 56 * 1024 * 1024

FFN_TM = 512
FFN_TF = 512
INPROJ_TM = 1024
INPROJ_TN = 1024
MERGE_TM = 512
ATTN_TK = 512
ATTN_COLS = 512
DIFF_TQ = ATTN_COLS // 2
MOBA_TQ = ATTN_COLS


def _params(*sem):
    return pltpu.CompilerParams(dimension_semantics=sem, vmem_limit_bytes=VMEM_LIMIT_BYTES)


def _rms(x, g, eps):
    return x * lax.rsqrt(jnp.mean(x * x, axis=-1, keepdims=True) + eps) * g


def _ffn_body(x_ref, pre_g_ref, wg_ref, wu_ref, wd_ref, post_g_ref, o_ref, h_ref, acc_ref):
    j = pl.program_id(1)

    @pl.when(j == 0)
    def _():
        h_ref[...] = _rms(x_ref[...], pre_g_ref[...], NORM_EPS).astype(BF16)

    h = h_ref[...]
    g = jnp.dot(h, wg_ref[...], preferred_element_type=F32)
    u = jnp.dot(h, wu_ref[...], preferred_element_type=F32)
    a = (g * jax.nn.sigmoid(g) * u).astype(BF16)
    d = jnp.dot(a, wd_ref[...], preferred_element_type=F32)

    @pl.when(j == 0)
    def _():
        acc_ref[...] = d

    @pl.when(j > 0)
    def _():
        acc_ref[...] += d

    @pl.when(j == pl.num_programs(1) - 1)
    def _():
        o_ref[...] = x_ref[...] + 0.5 * _rms(acc_ref[...], post_g_ref[...], NORM_EPS)


def _ffn(x, pre_g, w_gu, w_down, post_g, layer):
    m, d = x.shape
    f = w_down.shape[1]
    tm, tf = min(FFN_TM, m), min(FFN_TF, f)
    nf = f // tf
    return pl.pallas_call(
        _ffn_body,
        grid=(m // tm, nf),
        in_specs=[
            pl.BlockSpec((tm, d), lambda i, j: (i, 0)),
            pl.BlockSpec((None, 1, d), lambda i, j: (layer, 0, 0)),
            pl.BlockSpec((None, d, tf), lambda i, j: (layer, 0, j)),
            pl.BlockSpec((None, d, tf), lambda i, j: (layer, 0, j + nf)),
            pl.BlockSpec((None, tf, d), lambda i, j: (layer, j, 0)),
            pl.BlockSpec((None, 1, d), lambda i, j: (layer, 0, 0)),
        ],
        out_specs=pl.BlockSpec((tm, d), lambda i, j: (i, 0)),
        out_shape=jax.ShapeDtypeStruct((m, d), F32),
        scratch_shapes=[pltpu.VMEM((tm, d), BF16), pltpu.VMEM((tm, d), F32)],
        compiler_params=_params("arbitrary", "arbitrary"),
        name="ffn",
    )(x, pre_g, w_gu, w_gu, w_down, post_g)


def _rope_tables(seq, head_dim, period_lanes=LANES):
    rot = head_dim // ROT_FRAC_DIV
    half = rot // 2
    inv = jnp.power(ROPE_THETA, -jnp.arange(0, rot, 2, dtype=F32) / rot)
    ang = jnp.arange(seq, dtype=F32)[:, None] * inv[None, :]
    cos, sin = jnp.cos(ang), jnp.sin(ang)
    pad = head_dim - rot
    ones = jnp.ones((seq, pad), F32)
    zeros = jnp.zeros((seq, pad), F32)
    zh = jnp.zeros((seq, half), F32)
    c = jnp.concatenate([cos, cos, ones], axis=1)
    a = jnp.concatenate([-sin, zh, zeros], axis=1)
    b = jnp.concatenate([zh, sin, zeros], axis=1)
    reps = period_lanes // head_dim
    return jnp.concatenate([jnp.tile(t, (1, reps)) for t in (c, a, b)], axis=1)


def _inproj_body(x_ref, g_ref, w_ref, rd_ref, rm_ref, o_ref, h_ref, *, tn, d_model):
    j = pl.program_id(1)

    @pl.when(j == 0)
    def _():
        h_ref[...] = _rms(x_ref[...], g_ref[...], NORM_EPS).astype(BF16)

    acc = jnp.dot(h_ref[...], w_ref[...], preferred_element_type=F32)
    per_seg = ATTN_WIDTH // tn

    def seg(k):
        return (j >= k * per_seg) & (j < (k + 1) * per_seg)

    def rope_store(tab_ref, half, scale):
        c = tab_ref[:, 0:LANES]
        a = tab_ref[:, LANES:2 * LANES]
        b = tab_ref[:, 2 * LANES:3 * LANES]
        for s in range(tn // LANES):
            xs = acc[:, s * LANES:(s + 1) * LANES]
            r = xs * c + pltpu.roll(xs, LANES - half, 1) * a + pltpu.roll(xs, half, 1) * b
            o_ref[:, s * LANES:(s + 1) * LANES] = (r * scale).astype(BF16)

    d_half = DIFF_HEAD_DIM // ROT_FRAC_DIV // 2
    m_half = MOBA_HEAD_DIM // ROT_FRAC_DIV // 2

    @pl.when(seg(0))
    def _():
        rope_store(rd_ref, d_half, DIFF_HEAD_DIM ** -0.5)

    @pl.when(seg(1))
    def _():
        rope_store(rd_ref, d_half, 1.0)

    @pl.when(seg(3))
    def _():
        rope_store(rm_ref, m_half, MOBA_HEAD_DIM ** -0.5)

    @pl.when(seg(4))
    def _():
        rope_store(rm_ref, m_half, 1.0)

    @pl.when(seg(2) | seg(5))
    def _():
        o_ref[...] = acc.astype(BF16)

    @pl.when(j >= 6 * per_seg)
    def _():
        o_ref[...] = jax.nn.sigmoid(acc).astype(BF16)


def _inproj(x, g, w_in, rope_d, rope_m, layer, seq):
    m, d = x.shape
    width = w_in.shape[2]
    tm = min(INPROJ_TM, seq)
    tn = min(INPROJ_TN, ATTN_WIDTH)
    seq_tiles = seq // tm
    return pl.pallas_call(
        functools.partial(_inproj_body, tn=tn, d_model=d),
        grid=(m // tm, width // tn),
        in_specs=[
            pl.BlockSpec((tm, d), lambda i, j: (i, 0)),
            pl.BlockSpec((None, 1, d), lambda i, j: (layer, 0, 0)),
            pl.BlockSpec((None, d, tn), lambda i, j: (layer, 0, j)),
            pl.BlockSpec((tm, 3 * LANES), lambda i, j: (i % seq_tiles, 0)),
            pl.BlockSpec((tm, 3 * LANES), lambda i, j: (i % seq_tiles, 0)),
        ],
        out_specs=pl.BlockSpec((tm, tn), lambda i, j: (i, j)),
        out_shape=jax.ShapeDtypeStruct((m, width), BF16),
        scratch_shapes=[pltpu.VMEM((tm, d), BF16)],
        compiler_params=_params("arbitrary", "arbitrary"),
        name="inproj",
    )(x, g, w_in, rope_d, rope_m)


def _transpose_values(v_ref, vt_ref, seq, tk):
    for c in range(seq // tk):
        vt_ref[c] = v_ref[c * tk:(c + 1) * tk, :].astype(F32).T.astype(BF16)


def _softmax_step(st, vt_blk, m_ref, l_ref, acc_ref, first):
    m_blk = jnp.max(st, axis=0, keepdims=True)
    if first:
        m_new = m_blk
        p = jnp.exp(st - m_new)
        l_ref[...] = jnp.sum(p, axis=0, keepdims=True)
        acc_ref[...] = jnp.dot(vt_blk, p.astype(BF16), preferred_element_type=F32)
    else:
        m_old = m_ref[...]
        m_new = jnp.maximum(m_old, m_blk)
        alpha = jnp.exp(m_old - m_new)
        p = jnp.exp(st - m_new)
        l_ref[...] = alpha * l_ref[...] + jnp.sum(p, axis=0, keepdims=True)
        acc_ref[...] = alpha * acc_ref[...] + jnp.dot(
            vt_blk, p.astype(BF16), preferred_element_type=F32)
    m_ref[...] = m_new


def _diff_body(lq1_ref, lk1_ref, lq2_ref, lk2_ref, sg_ref, q_ref, k_ref, v_ref, o_ref,
               qt_ref, vt_ref, m_ref, l_ref, acc_ref, *, seq, tq, tk, lambda_init):
    qi = pl.program_id(2)
    cols = 2 * tq
    d = DIFF_HEAD_DIM

    @pl.when(qi == 0)
    def _():
        _transpose_values(v_ref, vt_ref, seq, tk)

    qt = q_ref[...].astype(F32).T
    row = lax.broadcasted_iota(jnp.int32, (2 * d, tq), 0)
    qt_ref[:, :tq] = jnp.where(row < d, qt, 0.0).astype(BF16)
    qt_ref[:, tq:] = jnp.where(row >= d, qt, 0.0).astype(BF16)

    def scores(kb):
        k_blk = k_ref[pl.ds(pl.multiple_of(kb * tk, tk), tk), :]
        return jnp.dot(k_blk, qt_ref[...], preferred_element_type=F32)

    kl = (qi * tq) // tk
    r = lax.broadcasted_iota(jnp.int32, (tk, cols), 0)
    c = lax.broadcasted_iota(jnp.int32, (tk, cols), 1)
    c = jnp.where(c >= tq, c - tq, c)
    causal = (kl * tk + r) <= (qi * tq + c)
    _softmax_step(jnp.where(causal, scores(kl), NEG_INF), vt_ref[kl], m_ref, l_ref, acc_ref, True)

    def past(kb, carry):
        _softmax_step(scores(kb), vt_ref[kb], m_ref, l_ref, acc_ref, False)
        return carry

    lax.fori_loop(0, kl, past, 0)

    lam = (jnp.exp(jnp.sum(lq1_ref[...] * lk1_ref[...], axis=-1, keepdims=True))
           - jnp.exp(jnp.sum(lq2_ref[...] * lk2_ref[...], axis=-1, keepdims=True))
           + lambda_init)
    acc = acc_ref[...] * (1.0 / l_ref[...])
    o = (acc[:, :tq] - lam * acc[:, tq:]).T
    o_ref[...] = (_rms(o, sg_ref[...], SUBLN_EPS) * (1.0 - lambda_init)).astype(BF16)


def _diff_attention(proj3, lq1, lk1, lq2, lk2, subln_g, layer, lambda_init):
    b, seq, _ = proj3.shape
    tq, tk = min(DIFF_TQ, seq), min(ATTN_TK, seq)
    hw = 2 * DIFF_HEAD_DIM
    kcol = ATTN_WIDTH // hw
    vcol = 2 * ATTN_WIDTH // hw
    lspec = pl.BlockSpec((None, 1, DIFF_HEAD_DIM), lambda bi, h, qi: (layer, 0, 0))
    return pl.pallas_call(
        functools.partial(_diff_body, seq=seq, tq=tq, tk=tk, lambda_init=lambda_init),
        grid=(b, DIFF_HEADS, seq // tq),
        in_specs=[
            lspec, lspec, lspec, lspec,
            pl.BlockSpec((None, 1, hw), lambda bi, h, qi: (layer, 0, 0)),
            pl.BlockSpec((None, tq, hw), lambda bi, h, qi: (bi, qi, h)),
            pl.BlockSpec((None, seq, hw), lambda bi, h, qi: (bi, 0, kcol + h)),
            pl.BlockSpec((None, seq, hw), lambda bi, h, qi: (bi, 0, vcol + h)),
        ],
        out_specs=pl.BlockSpec((None, tq, hw), lambda bi, h, qi: (bi, qi, h)),
        out_shape=jax.ShapeDtypeStruct((b, seq, ATTN_WIDTH), BF16),
        scratch_shapes=[
            pltpu.VMEM((hw, 2 * tq), BF16),
            pltpu.VMEM((seq // tk, hw, tk), BF16),
            pltpu.VMEM((1, 2 * tq), F32),
            pltpu.VMEM((1, 2 * tq), F32),
            pltpu.VMEM((hw, 2 * tq), F32),
        ],
        compiler_params=_params("arbitrary", "arbitrary", "arbitrary"),
        name="diff_attn",
    )(lq1, lk1, lq2, lk2, subln_g, proj3, proj3, proj3)


def _block_of(pos):
    return jnp.right_shift(pos, MOBA_BLOCK.bit_length() - 1)


def _moba_body(q_ref, k_ref, v_ref, o_ref, qt_ref, vt_ref, km_ref, mem_ref, m_ref, l_ref, acc_ref,
               *, seq, tq, tk):
    qi = pl.program_id(2)
    blk = MOBA_BLOCK
    nb = seq // blk
    per = tk // blk

    @pl.when(qi == 0)
    def _():
        _transpose_values(v_ref, vt_ref, seq, tk)
        n_i = lax.broadcasted_iota(jnp.int32, (nb, seq), 0)
        s_i = lax.broadcasted_iota(jnp.int32, (nb, seq), 1)
        ind = jnp.where(_block_of(s_i) == n_i, 1.0 / blk, 0.0).astype(BF16)
        kmean = jnp.dot(ind, k_ref[...], preferred_element_type=F32)
        hi = kmean.astype(BF16)
        km_ref[:nb, :] = hi
        km_ref[nb:, :] = (kmean - hi.astype(F32)).astype(BF16)

    qt_ref[...] = q_ref[...].astype(F32).T.astype(BF16)

    g2 = jnp.dot(km_ref[...], qt_ref[...], preferred_element_type=F32)
    g = g2[:nb, :] + g2[nb:, :]
    n_i = lax.broadcasted_iota(jnp.int32, (nb, tq), 0)
    c_i = lax.broadcasted_iota(jnp.int32, (nb, tq), 1)
    is_past = n_i < _block_of(qi * tq + c_i)
    gm = jnp.where(is_past, g, NEG_INF)
    member = jnp.zeros((nb, tq), F32)
    for _ in range(min(MOBA_TOPK, nb)):
        mx = jnp.max(gm, axis=0, keepdims=True)
        first = jnp.min(jnp.where(gm == mx, n_i, nb), axis=0, keepdims=True)
        sel = n_i == first
        member = jnp.where(sel & is_past, 1.0, member)
        gm = jnp.where(sel, REMOVED, gm)
    for n in range(nb):
        mem_ref[n] = member[n:n + 1, :]

    def scores(kb):
        k_blk = k_ref[pl.ds(pl.multiple_of(kb * tk, tk), tk), :]
        return jnp.dot(k_blk, qt_ref[...], preferred_element_type=F32)

    r = lax.broadcasted_iota(jnp.int32, (tk, tq), 0)
    c = lax.broadcasted_iota(jnp.int32, (tk, tq), 1)
    r_blk, c_blk = _block_of(r), _block_of(c)
    allowed = (r <= c) & (r_blk == c_blk)
    for t in range(per - 1):
        allowed = allowed | ((r_blk == t) & (c_blk > t) & (mem_ref[qi * per + t] > 0.0))
    _softmax_step(jnp.where(allowed, scores(qi), NEG_INF), vt_ref[qi], m_ref, l_ref, acc_ref, True)

    def past(kb, carry):
        st = scores(kb)
        st = jnp.concatenate(
            [jnp.where(mem_ref[kb * per + t] > 0.0, st[t * blk:(t + 1) * blk, :], NEG_INF)
             for t in range(per)], axis=0)
        _softmax_step(st, vt_ref[kb], m_ref, l_ref, acc_ref, False)
        return carry

    lax.fori_loop(0, qi, past, 0)

    o_ref[...] = (acc_ref[...] * (1.0 / l_ref[...])).T.astype(BF16)


def _moba_attention(proj3):
    b, seq, _ = proj3.shape
    tq = tk = min(MOBA_TQ, seq)
    hd = MOBA_HEAD_DIM
    nb = seq // MOBA_BLOCK
    qcol = 3 * ATTN_WIDTH // hd
    kcol = 4 * ATTN_WIDTH // hd
    vcol = 5 * ATTN_WIDTH // hd
    return pl.pallas_call(
        functools.partial(_moba_body, seq=seq, tq=tq, tk=tk),
        grid=(b, MOBA_HEADS, seq // tq),
        in_specs=[
            pl.BlockSpec((None, tq, hd), lambda bi, h, qi: (bi, qi, qcol + h)),
            pl.BlockSpec((None, seq, hd), lambda bi, h, qi: (bi, 0, kcol + h)),
            pl.BlockSpec((None, seq, hd), lambda bi, h, qi: (bi, 0, vcol + h)),
        ],
        out_specs=pl.BlockSpec((None, tq, hd), lambda bi, h, qi: (bi, qi, h)),
        out_shape=jax.ShapeDtypeStruct((b, seq, ATTN_WIDTH), BF16),
        scratch_shapes=[
            pltpu.VMEM((hd, tq), BF16),
            pltpu.VMEM((seq // tk, hd, tk), BF16),
            pltpu.VMEM((2 * nb, hd), BF16),
            pltpu.VMEM((nb, 1, tq), F32),
            pltpu.VMEM((1, tq), F32),
            pltpu.VMEM((1, tq), F32),
            pltpu.VMEM((hd, tq), F32),
        ],
        compiler_params=_params("arbitrary", "arbitrary", "arbitrary"),
        name="moba_attn",
    )(proj3, proj3, proj3)


def _merge_body(ya_ref, yb_ref, ga_ref, gb_ref, x_ref, wa_ref, wb_ref, wo_ref, g_ref, o_ref):
    ta = jnp.dot(ya_ref[...], wa_ref[...], preferred_element_type=F32)
    tb = jnp.dot(yb_ref[...], wb_ref[...], preferred_element_type=F32)
    merged = (ga_ref[...].astype(F32) * ta + gb_ref[...].astype(F32) * tb).astype(BF16)
    y = jnp.dot(merged, wo_ref[...], preferred_element_type=F32)
    o_ref[...] = x_ref[...] + _rms(y, g_ref[...], NORM_EPS)


def _merge(ya, yb, proj, x, w_a, w_b, w_o, g, layer):
    m, d = x.shape
    tm = min(MERGE_TM, m)
    gate_col = 6 * ATTN_WIDTH // d
    once = dict(pipeline_mode=pl.Buffered(1))
    return pl.pallas_call(
        _merge_body,
        grid=(m // tm,),
        in_specs=[
            pl.BlockSpec((tm, ATTN_WIDTH), lambda i: (i, 0)),
            pl.BlockSpec((tm, ATTN_WIDTH), lambda i: (i, 0)),
            pl.BlockSpec((tm, d), lambda i: (i, gate_col)),
            pl.BlockSpec((tm, d), lambda i: (i, gate_col + 1)),
            pl.BlockSpec((tm, d), lambda i: (i, 0)),
            pl.BlockSpec((None, ATTN_WIDTH, d), lambda i: (layer, 0, 0), **once),
            pl.BlockSpec((None, ATTN_WIDTH, d), lambda i: (layer, 0, 0), **once),
            pl.BlockSpec((None, d, d), lambda i: (layer, 0, 0), **once),
            pl.BlockSpec((None, 1, d), lambda i: (layer, 0, 0)),
        ],
        out_specs=pl.BlockSpec((tm, d), lambda i: (i, 0)),
        out_shape=jax.ShapeDtypeStruct((m, d), F32),
        compiler_params=_params("arbitrary"),
        name="merge",
    )(ya, yb, proj, proj, x, w_a, w_b, w_o, g)


def _trunk(x, ffn1_pre_g, ffn1_w_gu, ffn1_w_down, ffn1_post_g, mix_pre_g, w_in,
           diff_lq1, diff_lk1, diff_lq2, diff_lk2, diff_subln_g,
           w_branch_diff, w_branch_moba, w_out, mix_post_g,
           ffn2_pre_g, ffn2_w_gu, ffn2_w_down, ffn2_post_g):
    b, seq, d = x.shape
    depth = w_in.shape[0]
    assert (6 * ATTN_WIDTH) % d == 0 and seq % ATTN_TK == 0
    row = lambda p: p[:, None, :]
    bf = lambda w: w.astype(BF16)
    rope_d = _rope_tables(seq, DIFF_HEAD_DIM)
    rope_m = _rope_tables(seq, MOBA_HEAD_DIM)
    w1gu, w1d, w2gu, w2d = bf(ffn1_w_gu), bf(ffn1_w_down), bf(ffn2_w_gu), bf(ffn2_w_down)
    win, wa, wb, wo = bf(w_in), bf(w_branch_diff), bf(w_branch_moba), bf(w_out)
    xt = x.reshape(b * seq, d)
    for l in range(depth):
        lambda_init = 0.8 - 0.6 * math.exp(-0.3 * l)
        xt = _ffn(xt, row(ffn1_pre_g), w1gu, w1d, row(ffn1_post_g), l)
        proj = _inproj(xt, row(mix_pre_g), win, rope_d, rope_m, l, seq)
        proj3 = proj.reshape(b, seq, proj.shape[1])
        ya = _diff_attention(proj3, row(diff_lq1), row(diff_lk1), row(diff_lq2), row(diff_lk2),
                             row(diff_subln_g), l, lambda_init)
        yb = _moba_attention(proj3)
        xt = _merge(ya.reshape(b * seq, ATTN_WIDTH), yb.reshape(b * seq, ATTN_WIDTH), proj, xt,
                    wa, wb, wo, row(mix_post_g), l)
        xt = _ffn(xt, row(ffn2_pre_g), w2gu, w2d, row(ffn2_post_g), l)
    return xt.reshape(b, seq, d)


def kernel(x, ffn1_pre_g, ffn1_w_gu, ffn1_w_down, ffn1_post_g, mix_pre_g, w_in, diff_lq1, diff_lk1, diff_lq2, diff_lk2, diff_subln_g, w_branch_diff, w_branch_moba, w_out, mix_post_g, ffn2_pre_g, ffn2_w_gu, ffn2_w_down, ffn2_post_g):
    return _trunk(x, ffn1_pre_g, ffn1_w_gu, ffn1_w_down, ffn1_post_g, mix_pre_g, w_in,
                  diff_lq1, diff_lk1, diff_lq2, diff_lk2, diff_subln_g,
                  w_branch_diff, w_branch_moba, w_out, mix_post_g,
                  ffn2_pre_g, ffn2_w_gu, ffn2_w_down, ffn2_post_g)
```

```python
import functools
import math

import jax
import jax.numpy as jnp
from jax import lax
from jax.experimental import pallas as pl
from jax.experimental.pallas import tpu as pltpu

F32 = jnp.float32
BF16 = jnp.bfloat16

DIFF_HEADS = 8
DIFF_HEAD_DIM = 64
MOBA_HEADS = 8
MOBA_HEAD_DIM = 128
MOBA_BLOCK = 256
MOBA_TOPK = 3
ATTN_WIDTH = 1024
ROPE_THETA = 500000.0
ROT_FRAC_DIV = 4
NORM_EPS = 1e-6
SUBLN_EPS = 1e-5
NEG_INF = -1e30
REMOVED = -3e38
LOG2E = math.log2(math.e)

LANES = 128
SUBLANES = 8
VMEM_LIMIT_BYTES = 56 * 1024 * 1024

FFN_TM = 1024
FFN_TF = 256
INPROJ_TM = 1024
INPROJ_TN = 1024
MERGE_TM = 512
ATTN_TK = 512
ATTN_COLS = 512
DIFF_TQ = ATTN_COLS // 2
MOBA_TQ = ATTN_COLS

ONCE = dict(pipeline_mode=pl.Buffered(1))


def _params(*sem):
    return pltpu.CompilerParams(dimension_semantics=sem, vmem_limit_bytes=VMEM_LIMIT_BYTES)


def _rms(x, g, eps):
    return x * lax.rsqrt(jnp.mean(x * x, axis=-1, keepdims=True) + eps) * g


def _ffn_body(x_ref, pre_g_ref, wg_ref, wu_ref, wd_ref, post_g_ref, o_ref, h_ref, a_ref, acc_ref,
              *, nf):
    j = pl.program_id(1)

    def up(slot):
        h = h_ref[...]
        g = jnp.dot(h, wg_ref[...], preferred_element_type=F32)
        u = jnp.dot(h, wu_ref[...], preferred_element_type=F32)
        a_ref[slot] = (g * jax.nn.sigmoid(g) * u).astype(BF16)

    def down(slot):
        acc_ref[...] += jnp.dot(a_ref[slot], wd_ref[...], preferred_element_type=F32)

    @pl.when(j == 0)
    def _():
        h_ref[...] = _rms(x_ref[...], pre_g_ref[...], NORM_EPS).astype(BF16)
        acc_ref[...] = jnp.zeros_like(acc_ref)
        up(0)

    middle = (j > 0) & (j < nf)

    @pl.when(middle & (j % 2 == 1))
    def _():
        up(1)
        down(0)

    @pl.when(middle & (j % 2 == 0))
    def _():
        up(0)
        down(1)

    @pl.when(j == nf)
    def _():
        down((nf - 1) % 2)
        o_ref[...] = x_ref[...] + 0.5 * _rms(acc_ref[...], post_g_ref[...], NORM_EPS)


def _ffn(x, pre_g, w_gu, w_down, post_g, layer):
    m, d = x.shape
    f = w_down.shape[1]
    tm, tf = min(FFN_TM, m), min(FFN_TF, f)
    nf = f // tf
    up_blk = lambda j: jnp.minimum(j, nf - 1)
    down_blk = lambda j: jnp.maximum(j - 1, 0)
    return pl.pallas_call(
        functools.partial(_ffn_body, nf=nf),
        grid=(m // tm, nf + 1),
        in_specs=[
            pl.BlockSpec((tm, d), lambda i, j: (i, 0), **ONCE),
            pl.BlockSpec((None, 1, d), lambda i, j: (layer, 0, 0)),
            pl.BlockSpec((None, d, tf), lambda i, j: (layer, 0, up_blk(j))),
            pl.BlockSpec((None, d, tf), lambda i, j: (layer, 0, up_blk(j) + nf)),
            pl.BlockSpec((None, tf, d), lambda i, j: (layer, down_blk(j), 0)),
            pl.BlockSpec((None, 1, d), lambda i, j: (layer, 0, 0)),
        ],
        out_specs=pl.BlockSpec((tm, d), lambda i, j: (i, 0), **ONCE),
        out_shape=jax.ShapeDtypeStruct((m, d), F32),
        scratch_shapes=[
            pltpu.VMEM((tm, d), BF16),
            pltpu.VMEM((2, tm, tf), BF16),
            pltpu.VMEM((tm, d), F32),
        ],
        compiler_params=_params("arbitrary", "arbitrary"),
        name="ffn",
    )(x, pre_g, w_gu, w_gu, w_down, post_g)


def _inproj_body(x_ref, g_ref, w_ref, o_ref, h_ref):
    @pl.when(pl.program_id(1) == 0)
    def _():
        h_ref[...] = _rms(x_ref[...], g_ref[...], NORM_EPS).astype(BF16)

    o_ref[...] = jnp.dot(h_ref[...], w_ref[...], preferred_element_type=F32).astype(BF16)


def _inproj(x, g, w_in, layer):
    m, d = x.shape
    width = w_in.shape[2]
    tm = min(INPROJ_TM, m)
    tn = min(INPROJ_TN, ATTN_WIDTH)
    return pl.pallas_call(
        _inproj_body,
        grid=(m // tm, width // tn),
        in_specs=[
            pl.BlockSpec((tm, d), lambda i, j: (i, 0)),
            pl.BlockSpec((None, 1, d), lambda i, j: (layer, 0, 0)),
            pl.BlockSpec((None, d, tn), lambda i, j: (layer, 0, j)),
        ],
        out_specs=pl.BlockSpec((tm, tn), lambda i, j: (i, j)),
        out_shape=jax.ShapeDtypeStruct((m, width), BF16),
        scratch_shapes=[pltpu.VMEM((tm, d), BF16)],
        compiler_params=_params("arbitrary", "arbitrary"),
        name="inproj",
    )(x, g, w_in)


def _rope_tables_t(seq, head_dim):
    rot = head_dim // ROT_FRAC_DIV
    inv = jnp.power(ROPE_THETA, -jnp.arange(0, rot, 2, dtype=F32) / rot)
    ang = inv[:, None] * jnp.arange(seq, dtype=F32)[None, :]
    return jnp.cos(ang), jnp.sin(ang)


def _rope_t(xt, cos, sin, head_dim):
    half = cos.shape[0]
    assert half % SUBLANES == 0
    parts = []
    for base in range(0, xt.shape[0], head_dim):
        x1 = xt[base:base + half]
        x2 = xt[base + half:base + 2 * half]
        parts += [x1 * cos - x2 * sin, x2 * cos + x1 * sin, xt[base + 2 * half:base + head_dim]]
    return jnp.concatenate(parts, axis=0)


def _prepare_keys_values(k_ref, v_ref, cos_ref, sin_ref, kr_ref, vt_ref, seq, tk, head_dim):
    for c in range(seq // tk):
        rows = slice(c * tk, (c + 1) * tk)
        kt = _rope_t(k_ref[rows, :].astype(F32).T, cos_ref[:, rows], sin_ref[:, rows], head_dim)
        kr_ref[rows, :] = kt.T.astype(BF16)
        vt_ref[c] = v_ref[rows, :].astype(F32).T.astype(BF16)


def _attend(n_past, scores, past_mask, diag_mask, vt_ref, s_ref, mb_ref, m_ref, l_ref, acc_ref):
    def produce(kb, slot, past):
        st = scores(kb)
        if past:
            st = past_mask(kb, st)
            mb_ref[slot] = jnp.max(st, axis=0, keepdims=True)
        s_ref[slot] = st

    def consume(kb, slot, diag):
        st = s_ref[slot]
        vt_blk = vt_ref[kb]
        if diag:
            st = diag_mask(st)
            m_new = jnp.max(st, axis=0, keepdims=True)
            p = jnp.exp2(st - m_new)
            l_ref[...] = jnp.sum(p, axis=0, keepdims=True)
            acc_ref[...] = jnp.dot(vt_blk, p.astype(BF16), preferred_element_type=F32)
        else:
            m_old = m_ref[...]
            m_new = jnp.maximum(m_old, mb_ref[slot])
            alpha = jnp.exp2(m_old - m_new)
            p = jnp.exp2(st - m_new)
            l_ref[...] = alpha * l_ref[...] + jnp.sum(p, axis=0, keepdims=True)
            acc_ref[...] = alpha * acc_ref[...] + jnp.dot(
                vt_blk, p.astype(BF16), preferred_element_type=F32)
        m_ref[...] = m_new

    produce(n_past, 0, False)

    @pl.when(n_past > 0)
    def _():
        produce(0, 1, True)

    consume(n_past, 0, True)

    @pl.when(n_past > 0)
    def _():
        rest = n_past - 1
        pairs = rest // 2

        def pair(t, carry):
            c = 2 * t + 1
            produce(c, 0, True)
            consume(c - 1, 1, False)
            produce(c + 1, 1, True)
            consume(c, 0, False)
            return carry

        lax.fori_loop(0, pairs, pair, 0)
        last = 2 * pairs

        @pl.when(rest % 2 == 1)
        def _():
            produce(last + 1, 0, True)
            consume(last, 1, False)
            consume(last + 1, 0, False)

        @pl.when(rest % 2 == 0)
        def _():
            consume(last, 1, False)


def _diff_body(lq1_ref, lk1_ref, lq2_ref, lk2_ref, sg_ref, cq_ref, sq_ref, ck_ref, sk_ref,
               q_ref, k_ref, v_ref, o_ref,
               qt_ref, kr_ref, vt_ref, s_ref, mb_ref, m_ref, l_ref, acc_ref,
               *, seq, tq, tk, lambda_init):
    qi = pl.program_id(2)
    cols = 2 * tq
    d = DIFF_HEAD_DIM

    @pl.when(qi == 0)
    def _():
        _prepare_keys_values(k_ref, v_ref, ck_ref, sk_ref, kr_ref, vt_ref, seq, tk, d)

    qt = _rope_t(q_ref[...].astype(F32).T, cq_ref[...], sq_ref[...], d) * (d ** -0.5 * LOG2E)
    row = lax.broadcasted_iota(jnp.int32, (2 * d, tq), 0)
    qt_ref[:, :tq] = jnp.where(row < d, qt, 0.0).astype(BF16)
    qt_ref[:, tq:] = jnp.where(row >= d, qt, 0.0).astype(BF16)

    def scores(kb):
        k_blk = kr_ref[pl.ds(pl.multiple_of(kb * tk, tk), tk), :]
        return jnp.dot(k_blk, qt_ref[...], preferred_element_type=F32)

    kl = (qi * tq) // tk

    def diag_mask(st):
        r = lax.broadcasted_iota(jnp.int32, (tk, cols), 0)
        c = lax.broadcasted_iota(jnp.int32, (tk, cols), 1)
        c = jnp.where(c >= tq, c - tq, c)
        return jnp.where((kl * tk + r) <= (qi * tq + c), st, NEG_INF)

    _attend(kl, scores, lambda kb, st: st, diag_mask, vt_ref, s_ref, mb_ref, m_ref, l_ref, acc_ref)

    lam = (jnp.exp(jnp.sum(lq1_ref[...] * lk1_ref[...], axis=-1, keepdims=True))
           - jnp.exp(jnp.sum(lq2_ref[...] * lk2_ref[...], axis=-1, keepdims=True))
           + lambda_init)
    acc = acc_ref[...] * (1.0 / l_ref[...])
    o = (acc[:, :tq] - lam * acc[:, tq:]).T
    o_ref[...] = (_rms(o, sg_ref[...], SUBLN_EPS) * (1.0 - lambda_init)).astype(BF16)


def _attn_scratch(seq, tk, hw, cols):
    return [
        pltpu.VMEM((hw, cols), BF16),
        pltpu.VMEM((seq, hw), BF16),
        pltpu.VMEM((seq // tk, hw, tk), BF16),
        pltpu.VMEM((2, tk, cols), F32),
        pltpu.VMEM((2, 1, cols), F32),
        pltpu.VMEM((1, cols), F32),
        pltpu.VMEM((1, cols), F32),
        pltpu.VMEM((hw, cols), F32),
    ]


def _diff_attention(proj3, cos_t, sin_t, lq1, lk1, lq2, lk2, subln_g, layer, lambda_init):
    b, seq, _ = proj3.shape
    tq, tk = min(DIFF_TQ, seq), min(ATTN_TK, seq)
    hw = 2 * DIFF_HEAD_DIM
    half = cos_t.shape[0]
    kcol = ATTN_WIDTH // hw
    vcol = 2 * ATTN_WIDTH // hw
    lspec = pl.BlockSpec((None, 1, DIFF_HEAD_DIM), lambda bi, h, qi: (layer, 0, 0))
    tab_q = pl.BlockSpec((half, tq), lambda bi, h, qi: (0, qi))
    tab_k = pl.BlockSpec((half, seq), lambda bi, h, qi: (0, 0))
    return pl.pallas_call(
        functools.partial(_diff_body, seq=seq, tq=tq, tk=tk, lambda_init=lambda_init),
        grid=(b, DIFF_HEADS, seq // tq),
        in_specs=[
            lspec, lspec, lspec, lspec,
            pl.BlockSpec((None, 1, hw), lambda bi, h, qi: (layer, 0, 0)),
            tab_q, tab_q, tab_k, tab_k,
            pl.BlockSpec((None, tq, hw), lambda bi, h, qi: (bi, qi, h)),
            pl.BlockSpec((None, seq, hw), lambda bi, h, qi: (bi, 0, kcol + h)),
            pl.BlockSpec((None, seq, hw), lambda bi, h, qi: (bi, 0, vcol + h)),
        ],
        out_specs=pl.BlockSpec((None, tq, hw), lambda bi, h, qi: (bi, qi, h)),
        out_shape=jax.ShapeDtypeStruct((b, seq, ATTN_WIDTH), BF16),
        scratch_shapes=_attn_scratch(seq, tk, hw, 2 * tq),
        compiler_params=_params("arbitrary", "arbitrary", "arbitrary"),
        name="diff_attn",
    )(lq1, lk1, lq2, lk2, subln_g, cos_t, sin_t, cos_t, sin_t, proj3, proj3, proj3)


def _block_of(pos):
    return jnp.right_shift(pos, MOBA_BLOCK.bit_length() - 1)


def _moba_body(cq_ref, sq_ref, ck_ref, sk_ref, q_ref, k_ref, v_ref, o_ref,
               qt_ref, kr_ref, vt_ref, s_ref, mb_ref, m_ref, l_ref, acc_ref, km_ref, mem_ref,
               *, seq, tq, tk):
    qi = pl.program_id(2)
    blk = MOBA_BLOCK
    hd = MOBA_HEAD_DIM
    nb = seq // blk
    per = tk // blk

    @pl.when(qi == 0)
    def _():
        _prepare_keys_values(k_ref, v_ref, ck_ref, sk_ref, kr_ref, vt_ref, seq, tk, hd)
        n_i = lax.broadcasted_iota(jnp.int32, (nb, seq), 0)
        s_i = lax.broadcasted_iota(jnp.int32, (nb, seq), 1)
        ind = jnp.where(_block_of(s_i) == n_i, 1.0 / blk, 0.0).astype(BF16)
        kmean = jnp.dot(ind, kr_ref[...], preferred_element_type=F32)
        hi = kmean.astype(BF16)
        km_ref[:nb, :] = hi
        km_ref[nb:, :] = (kmean - hi.astype(F32)).astype(BF16)

    qt = _rope_t(q_ref[...].astype(F32).T, cq_ref[...], sq_ref[...], hd) * (hd ** -0.5 * LOG2E)
    qt_ref[...] = qt.astype(BF16)

    g2 = jnp.dot(km_ref[...], qt_ref[...], preferred_element_type=F32)
    g = g2[:nb, :] + g2[nb:, :]
    n_i = lax.broadcasted_iota(jnp.int32, (nb, tq), 0)
    c_i = lax.broadcasted_iota(jnp.int32, (nb, tq), 1)
    is_past = n_i < _block_of(qi * tq + c_i)
    gm = jnp.where(is_past, g, NEG_INF)
    member = jnp.zeros((nb, tq), F32)
    for _ in range(min(MOBA_TOPK, nb)):
        mx = jnp.max(gm, axis=0, keepdims=True)
        first = jnp.min(jnp.where(gm == mx, n_i, nb), axis=0, keepdims=True)
        sel = n_i == first
        member = jnp.where(sel & is_past, 1.0, member)
        gm = jnp.where(sel, REMOVED, gm)
    for n in range(nb):
        mem_ref[n] = member[n:n + 1, :]

    def scores(kb):
        k_blk = kr_ref[pl.ds(pl.multiple_of(kb * tk, tk), tk), :]
        return jnp.dot(k_blk, qt_ref[...], preferred_element_type=F32)

    def past_mask(kb, st):
        return jnp.concatenate(
            [jnp.where(mem_ref[kb * per + t] > 0.0, st[t * blk:(t + 1) * blk, :], NEG_INF)
             for t in range(per)], axis=0)

    def diag_mask(st):
        r = lax.broadcasted_iota(jnp.int32, (tk, tq), 0)
        c = lax.broadcasted_iota(jnp.int32, (tk, tq), 1)
        r_blk, c_blk = _block_of(r), _block_of(c)
        allowed = (r <= c) & (r_blk == c_blk)
        for t in range(per - 1):
            allowed = allowed | ((r_blk == t) & (c_blk > t) & (mem_ref[qi * per + t] > 0.0))
        return jnp.where(allowed, st, NEG_INF)

    _attend(qi, scores, past_mask, diag_mask, vt_ref, s_ref, mb_ref, m_ref, l_ref, acc_ref)

    o_ref[...] = (acc_ref[...] * (1.0 / l_ref[...])).T.astype(BF16)


def _moba_attention(proj3, cos_t, sin_t):
    b, seq, _ = proj3.shape
    tq = tk = min(MOBA_TQ, seq)
    hd = MOBA_HEAD_DIM
    nb = seq // MOBA_BLOCK
    half = cos_t.shape[0]
    qcol = 3 * ATTN_WIDTH // hd
    kcol = 4 * ATTN_WIDTH // hd
    vcol = 5 * ATTN_WIDTH // hd
    tab_q = pl.BlockSpec((half, tq), lambda bi, h, qi: (0, qi))
    tab_k = pl.BlockSpec((half, seq), lambda bi, h, qi: (0, 0))
    return pl.pallas_call(
        functools.partial(_moba_body, seq=seq, tq=tq, tk=tk),
        grid=(b, MOBA_HEADS, seq // tq),
        in_specs=[
            tab_q, tab_q, tab_k, tab_k,
            pl.BlockSpec((None, tq, hd), lambda bi, h, qi: (bi, qi, qcol + h)),
            pl.BlockSpec((None, seq, hd), lambda bi, h, qi: (bi, 0, kcol + h)),
            pl.BlockSpec((None, seq, hd), lambda bi, h, qi: (bi, 0, vcol + h)),
        ],
        out_specs=pl.BlockSpec((None, tq, hd), lambda bi, h, qi: (bi, qi, h)),
        out_shape=jax.ShapeDtypeStruct((b, seq, ATTN_WIDTH), BF16),
        scratch_shapes=_attn_scratch(seq, tk, hd, tq) + [
            pltpu.VMEM((2 * nb, hd), BF16),
            pltpu.VMEM((nb, 1, tq), F32),
        ],
        compiler_params=_params("arbitrary", "arbitrary", "arbitrary"),
        name="moba_attn",
    )(cos_t, sin_t, cos_t, sin_t, proj3, proj3, proj3)


def _merge_body(ya_ref, yb_ref, ga_ref, gb_ref, x_ref, wa_ref, wb_ref, wo_ref, g_ref, o_ref):
    ta = jnp.dot(ya_ref[...], wa_ref[...], preferred_element_type=F32)
    tb = jnp.dot(yb_ref[...], wb_ref[...], preferred_element_type=F32)
    gate_a = jax.nn.sigmoid(ga_ref[...].astype(F32))
    gate_b = jax.nn.sigmoid(gb_ref[...].astype(F32))
    merged = (gate_a * ta + gate_b * tb).astype(BF16)
    y = jnp.dot(merged, wo_ref[...], preferred_element_type=F32)
    o_ref[...] = x_ref[...] + _rms(y, g_ref[...], NORM_EPS)


def _merge(ya, yb, proj, x, w_a, w_b, w_o, g, layer):
    m, d = x.shape
    tm = min(MERGE_TM, m)
    gate_col = 6 * ATTN_WIDTH // d
    return pl.pallas_call(
        _merge_body,
        grid=(m // tm,),
        in_specs=[
            pl.BlockSpec((tm, ATTN_WIDTH), lambda i: (i, 0)),
            pl.BlockSpec((tm, ATTN_WIDTH), lambda i: (i, 0)),
            pl.BlockSpec((tm, d), lambda i: (i, gate_col)),
            pl.BlockSpec((tm, d), lambda i: (i, gate_col + 1)),
            pl.BlockSpec((tm, d), lambda i: (i, 0)),
            pl.BlockSpec((None, ATTN_WIDTH, d), lambda i: (layer, 0, 0), **ONCE),
            pl.BlockSpec((None, ATTN_WIDTH, d), lambda i: (layer, 0, 0), **ONCE),
            pl.BlockSpec((None, d, d), lambda i: (layer, 0, 0), **ONCE),
            pl.BlockSpec((None, 1, d), lambda i: (layer, 0, 0)),
        ],
        out_specs=pl.BlockSpec((tm, d), lambda i: (i, 0)),
        out_shape=jax.ShapeDtypeStruct((m, d), F32),
        compiler_params=_params("arbitrary"),
        name="merge",
    )(ya, yb, proj, proj, x, w_a, w_b, w_o, g)


def _trunk(x, ffn1_pre_g, ffn1_w_gu, ffn1_w_down, ffn1_post_g, mix_pre_g, w_in,
           diff_lq1, diff_lk1, diff_lq2, diff_lk2, diff_subln_g,
           w_branch_diff, w_branch_moba, w_out, mix_post_g,
           ffn2_pre_g, ffn2_w_gu, ffn2_w_down, ffn2_post_g):
    b, seq, d = x.shape
    depth = w_in.shape[0]
    assert (6 * ATTN_WIDTH) % d == 0 and seq % ATTN_TK == 0
    row = lambda p: p[:, None, :]
    bf = lambda w: w.astype(BF16)
    cos_d, sin_d = _rope_tables_t(seq, DIFF_HEAD_DIM)
    cos_m, sin_m = _rope_tables_t(seq, MOBA_HEAD_DIM)
    w1gu, w1d, w2gu, w2d = bf(ffn1_w_gu), bf(ffn1_w_down), bf(ffn2_w_gu), bf(ffn2_w_down)
    win, wa, wb, wo = bf(w_in), bf(w_branch_diff), bf(w_branch_moba), bf(w_out)
    xt = x.reshape(b * seq, d)
    for l in range(depth):
        lambda_init = 0.8 - 0.6 * math.exp(-0.3 * l)
        xt = _ffn(xt, row(ffn1_pre_g), w1gu, w1d, row(ffn1_post_g), l)
        proj = _inproj(xt, row(mix_pre_g), win, l)
        proj3 = proj.reshape(b, seq, proj.shape[1])
        ya = _diff_attention(proj3, cos_d, sin_d, row(diff_lq1), row(diff_lk1), row(diff_lq2),
                             row(diff_lk2), row(diff_subln_g), l, lambda_init)
        yb = _moba_attention(proj3, cos_m, sin_m)
        xt = _merge(ya.reshape(b * seq, ATTN_WIDTH), yb.reshape(b * seq, ATTN_WIDTH), proj, xt,
                    wa, wb, wo, row(mix_post_g), l)
        xt = _ffn(xt, row(ffn2_pre_g), w2gu, w2d, row(ffn2_post_g), l)
    return xt.reshape(b, seq, d)


def kernel(x, ffn1_pre_g, ffn1_w_gu, ffn1_w_down, ffn1_post_g, mix_pre_g, w_in, diff_lq1, diff_lk1, diff_lq2, diff_lk2, diff_subln_g, w_branch_diff, w_branch_moba, w_out, mix_post_g, ffn2_pre_g, ffn2_w_gu, ffn2_w_down, ffn2_post_g):
    return _trunk(x, ffn1_pre_g, ffn1_w_gu, ffn1_w_down, ffn1_post_g, mix_pre_g, w_in,
                  diff_lq1, diff_lk1, diff_lq2, diff_lk2, diff_subln_g,
                  w_branch_diff, w_branch_moba, w_out, mix_post_g,
                  ffn2_pre_g, ffn2_w_gu, ffn2_w_down, ffn2_post_g)
```

```python
import functools
import math
from typing import Any, Callable, NamedTuple

import jax
import jax.numpy as jnp
from jax import lax
from jax.experimental import pallas as pl
from jax.experimental.pallas import tpu as pltpu

F32 = jnp.float32
BF16 = jnp.bfloat16

DIFF_HEADS = 8
DIFF_HEAD_DIM = 64
MOBA_HEADS = 8
MOBA_HEAD_DIM = 128
MOBA_BLOCK = 256
MOBA_TOPK = 3
ATTN_WIDTH = 1024
HEAD_LANES = 128
ROPE_THETA = 500000.0
ROT_FRAC_DIV = 4
NORM_EPS = 1e-6
SUBLN_EPS = 1e-5
NEG_INF = -1e30
REMOVED = -3e38
LOG2E = math.log2(math.e)

SUBLANES = 8
VMEM_LIMIT_BYTES = 56 * 1024 * 1024

FFN_TM = 1024
FFN_TF = 512
INPROJ_TM = 1024
INPROJ_TN = 1024
MERGE_TM = 512
ATTN_TK = 512
ATTN_COLS = 512
ATTN_HEADS_PER_STEP = 4
DIFF_TQ = ATTN_COLS // 2
MOBA_TQ = ATTN_COLS

ONCE = dict(pipeline_mode=pl.Buffered(1))


def _params(*sem):
    return pltpu.CompilerParams(dimension_semantics=sem, vmem_limit_bytes=VMEM_LIMIT_BYTES)


def _rms(x, g, eps):
    return x * lax.rsqrt(jnp.mean(x * x, axis=-1, keepdims=True) + eps) * g


def _ffn_body(x_ref, pre_g_ref, wg_ref, wu_ref, wd_ref, post_g_ref, o_ref, h_ref, a_ref, *, nf):
    j = pl.program_id(1)

    def up(slot):
        h = h_ref[...]
        g = jnp.dot(h, wg_ref[...], preferred_element_type=F32)
        u = jnp.dot(h, wu_ref[...], preferred_element_type=F32)
        a_ref[slot] = (g * jax.nn.sigmoid(g) * u).astype(BF16)

    def down(slot):
        o_ref[...] += jnp.dot(a_ref[slot], wd_ref[...], preferred_element_type=F32)

    @pl.when(j == 0)
    def _():
        h_ref[...] = _rms(x_ref[...], pre_g_ref[...], NORM_EPS).astype(BF16)
        o_ref[...] = jnp.zeros_like(o_ref)
        up(0)

    middle = (j > 0) & (j < nf)

    @pl.when(middle & (j % 2 == 1))
    def _():
        up(1)
        down(0)

    @pl.when(middle & (j % 2 == 0))
    def _():
        up(0)
        down(1)

    @pl.when(j == nf)
    def _():
        down((nf - 1) % 2)
        o_ref[...] = x_ref[...] + 0.5 * _rms(o_ref[...], post_g_ref[...], NORM_EPS)


def _ffn(x, pre_g, w_gu, w_down, post_g, layer):
    m, d = x.shape
    f = w_down.shape[1]
    tm, tf = min(FFN_TM, m), min(FFN_TF, f)
    nf = f // tf
    up_blk = lambda j: jnp.minimum(j, nf - 1)
    down_blk = lambda j: jnp.maximum(j - 1, 0)
    return pl.pallas_call(
        functools.partial(_ffn_body, nf=nf),
        grid=(m // tm, nf + 1),
        in_specs=[
            pl.BlockSpec((tm, d), lambda i, j: (i, 0)),
            pl.BlockSpec((None, 1, d), lambda i, j: (layer, 0, 0)),
            pl.BlockSpec((None, d, tf), lambda i, j: (layer, 0, up_blk(j))),
            pl.BlockSpec((None, d, tf), lambda i, j: (layer, 0, up_blk(j) + nf)),
            pl.BlockSpec((None, tf, d), lambda i, j: (layer, down_blk(j), 0)),
            pl.BlockSpec((None, 1, d), lambda i, j: (layer, 0, 0)),
        ],
        out_specs=pl.BlockSpec((tm, d), lambda i, j: (i, 0)),
        out_shape=jax.ShapeDtypeStruct((m, d), F32),
        scratch_shapes=[
            pltpu.VMEM((tm, d), BF16),
            pltpu.VMEM((2, tm, tf), BF16),
        ],
        compiler_params=_params("arbitrary", "arbitrary"),
        name="ffn",
    )(x, pre_g, w_gu, w_gu, w_down, post_g)


def _inproj_body(x_ref, g_ref, w_ref, o_ref, h_ref):
    @pl.when(pl.program_id(1) == 0)
    def _():
        h_ref[...] = _rms(x_ref[...], g_ref[...], NORM_EPS).astype(BF16)

    o_ref[...] = jnp.dot(h_ref[...], w_ref[...], preferred_element_type=F32).astype(BF16)


def _inproj(x, g, w_in, layer):
    m, d = x.shape
    width = w_in.shape[2]
    tm = min(INPROJ_TM, m)
    tn = min(INPROJ_TN, ATTN_WIDTH)
    return pl.pallas_call(
        _inproj_body,
        grid=(m // tm, width // tn),
        in_specs=[
            pl.BlockSpec((tm, d), lambda i, j: (i, 0)),
            pl.BlockSpec((None, 1, d), lambda i, j: (layer, 0, 0)),
            pl.BlockSpec((None, d, tn), lambda i, j: (layer, 0, j)),
        ],
        out_specs=pl.BlockSpec((tm, tn), lambda i, j: (i, j)),
        out_shape=jax.ShapeDtypeStruct((m, width), BF16),
        scratch_shapes=[pltpu.VMEM((tm, d), BF16)],
        compiler_params=_params("arbitrary", "arbitrary"),
        name="inproj",
    )(x, g, w_in)


def _rope_tables_t(seq, head_dim):
    rot = head_dim // ROT_FRAC_DIV
    inv = jnp.power(ROPE_THETA, -jnp.arange(0, rot, 2, dtype=F32) / rot)
    ang = inv[:, None] * jnp.arange(seq, dtype=F32)[None, :]
    return jnp.cos(ang), jnp.sin(ang)


def _rope_t(xt, cos, sin, head_dim):
    half = cos.shape[0]
    assert half % SUBLANES == 0
    parts = []
    for base in range(0, xt.shape[0], head_dim):
        x1 = xt[base:base + half]
        x2 = xt[base + half:base + 2 * half]
        parts += [x1 * cos - x2 * sin, x2 * cos + x1 * sin, xt[base + 2 * half:base + head_dim]]
    return jnp.concatenate(parts, axis=0)


class _Stream(NamedTuple):
    scores: Callable[[Any], Any]
    past_mask: Callable[[Any, Any], Any]
    diag_mask: Callable[[Any], Any]
    vt: Any
    s: Any
    mb: Any
    m: Any
    l: Any
    acc: Any


def _head_scratch(heads, seq, tk, cols):
    hw = HEAD_LANES
    return [
        pltpu.VMEM((heads, hw, cols), BF16),
        pltpu.VMEM((heads, seq, hw), BF16),
        pltpu.VMEM((heads, seq // tk, hw, tk), BF16),
        pltpu.VMEM((heads, 2, tk, cols), F32),
        pltpu.VMEM((heads, 2, 1, cols), F32),
        pltpu.VMEM((heads, 1, cols), F32),
        pltpu.VMEM((heads, 1, cols), F32),
        pltpu.VMEM((heads, hw, cols), F32),
    ]


def _prepare_keys_values(k_ref, v_ref, cos_ref, sin_ref, kr_ref, vt_ref, lanes, seq, tk, head_dim):
    for c in range(seq // tk):
        rows = slice(c * tk, (c + 1) * tk)
        kt = _rope_t(k_ref[rows, lanes].astype(F32).T, cos_ref[:, rows], sin_ref[:, rows], head_dim)
        kr_ref[rows, :] = kt.T.astype(BF16)
        vt_ref[c] = v_ref[rows, lanes].astype(F32).T.astype(BF16)


def _attend(n_past, streams):
    def produce(kb, slot, past):
        for st in streams:
            sc = st.scores(kb)
            if past:
                sc = st.past_mask(kb, sc)
                st.mb[slot] = jnp.max(sc, axis=0, keepdims=True)
            st.s[slot] = sc

    def consume(kb, slot, diag):
        for st in streams:
            sc = st.s[slot]
            vt_blk = st.vt[kb]
            if diag:
                sc = st.diag_mask(sc)
                m_new = jnp.max(sc, axis=0, keepdims=True)
                p = jnp.exp2(sc - m_new)
                st.l[...] = jnp.sum(p, axis=0, keepdims=True)
                st.acc[...] = jnp.dot(vt_blk, p.astype(BF16), preferred_element_type=F32)
            else:
                m_old = st.m[...]
                m_new = jnp.maximum(m_old, st.mb[slot])
                alpha = jnp.exp2(m_old - m_new)
                p = jnp.exp2(sc - m_new)
                st.l[...] = alpha * st.l[...] + jnp.sum(p, axis=0, keepdims=True)
                st.acc[...] = alpha * st.acc[...] + jnp.dot(
                    vt_blk, p.astype(BF16), preferred_element_type=F32)
            st.m[...] = m_new

    produce(n_past, 0, False)

    @pl.when(n_past == 0)
    def _():
        consume(n_past, 0, True)

    @pl.when(n_past > 0)
    def _():
        produce(0, 1, True)
        consume(n_past, 0, True)
        rest = n_past - 1
        pairs = rest // 2

        def pair(t, carry):
            c = 2 * t + 1
            produce(c, 0, True)
            consume(c - 1, 1, False)
            produce(c + 1, 1, True)
            consume(c, 0, False)
            return carry

        lax.fori_loop(0, pairs, pair, 0)
        last = 2 * pairs

        @pl.when(rest % 2 == 1)
        def _():
            produce(last + 1, 0, True)
            consume(last, 1, False)
            consume(last + 1, 0, False)

        @pl.when(rest % 2 == 0)
        def _():
            consume(last, 1, False)


def _diff_causal_bias(tq, tk):
    p = lax.broadcasted_iota(jnp.int32, (tk // tq, tk, 2 * tq), 0)
    r = lax.broadcasted_iota(jnp.int32, (tk // tq, tk, 2 * tq), 1)
    c = lax.broadcasted_iota(jnp.int32, (tk // tq, tk, 2 * tq), 2) % tq
    return jnp.where(r <= p * tq + c, 0.0, NEG_INF).astype(F32)


def _diff_body(lq1_ref, lk1_ref, lq2_ref, lk2_ref, sg_ref, cq_ref, sq_ref, ck_ref, sk_ref, bias_ref,
               q_ref, k_ref, v_ref, o_ref,
               qt_ref, kr_ref, vt_ref, s_ref, mb_ref, m_ref, l_ref, acc_ref,
               *, heads, seq, tq, tk, lambda_init):
    qi = pl.program_id(2)
    cols = 2 * tq
    d = DIFF_HEAD_DIM
    head_lanes = [slice(hh * HEAD_LANES, (hh + 1) * HEAD_LANES) for hh in range(heads)]

    @pl.when(qi == 0)
    def _():
        for hh in range(heads):
            _prepare_keys_values(k_ref, v_ref, ck_ref, sk_ref, kr_ref.at[hh], vt_ref.at[hh],
                                 head_lanes[hh], seq, tk, d)

    row = lax.broadcasted_iota(jnp.int32, (2 * d, tq), 0)
    for hh in range(heads):
        qt = _rope_t(q_ref[:, head_lanes[hh]].astype(F32).T, cq_ref[...], sq_ref[...], d)
        qt = qt * (d ** -0.5 * LOG2E)
        qt_ref[hh, :, :tq] = jnp.where(row < d, qt, 0.0).astype(BF16)
        qt_ref[hh, :, tq:] = jnp.where(row >= d, qt, 0.0).astype(BF16)

    kl = (qi * tq) // tk

    def diag_mask(sc):
        return sc + bias_ref[qi % (tk // tq)]

    def make_stream(hh):
        def scores(kb):
            k_blk = kr_ref[hh, pl.ds(pl.multiple_of(kb * tk, tk), tk), :]
            return jnp.dot(k_blk, qt_ref[hh], preferred_element_type=F32)

        return _Stream(scores, lambda kb, sc: sc, diag_mask, vt_ref.at[hh], s_ref.at[hh],
                       mb_ref.at[hh], m_ref.at[hh], l_ref.at[hh], acc_ref.at[hh])

    _attend(kl, [make_stream(hh) for hh in range(heads)])

    lam = (jnp.exp(jnp.sum(lq1_ref[...] * lk1_ref[...], axis=-1, keepdims=True))
           - jnp.exp(jnp.sum(lq2_ref[...] * lk2_ref[...], axis=-1, keepdims=True))
           + lambda_init)
    for hh in range(heads):
        acc = acc_ref[hh] * (1.0 / l_ref[hh])
        o = (acc[:, :tq] - lam * acc[:, tq:]).T
        o_ref[:, head_lanes[hh]] = (
            _rms(o, sg_ref[...], SUBLN_EPS) * (1.0 - lambda_init)).astype(BF16)


def _diff_attention(proj3, cos_t, sin_t, lq1, lk1, lq2, lk2, subln_g, layer, lambda_init):
    b, seq, _ = proj3.shape
    tq, tk = min(DIFF_TQ, seq), min(ATTN_TK, seq)
    heads = ATTN_HEADS_PER_STEP
    bw = heads * HEAD_LANES
    half = cos_t.shape[0]
    kcol = ATTN_WIDTH // bw
    vcol = 2 * ATTN_WIDTH // bw
    lspec = pl.BlockSpec((None, 1, DIFF_HEAD_DIM), lambda bi, h, qi: (layer, 0, 0))
    tab_q = pl.BlockSpec((half, tq), lambda bi, h, qi: (0, qi))
    tab_k = pl.BlockSpec((half, seq), lambda bi, h, qi: (0, 0))
    return pl.pallas_call(
        functools.partial(_diff_body, heads=heads, seq=seq, tq=tq, tk=tk, lambda_init=lambda_init),
        grid=(b, DIFF_HEADS // heads, seq // tq),
        in_specs=[
            lspec, lspec, lspec, lspec,
            pl.BlockSpec((None, 1, HEAD_LANES), lambda bi, h, qi: (layer, 0, 0)),
            tab_q, tab_q, tab_k, tab_k,
            pl.BlockSpec((tk // tq, tk, 2 * tq), lambda bi, h, qi: (0, 0, 0), **ONCE),
            pl.BlockSpec((None, tq, bw), lambda bi, h, qi: (bi, qi, h)),
            pl.BlockSpec((None, seq, bw), lambda bi, h, qi: (bi, 0, kcol + h)),
            pl.BlockSpec((None, seq, bw), lambda bi, h, qi: (bi, 0, vcol + h)),
        ],
        out_specs=pl.BlockSpec((None, tq, bw), lambda bi, h, qi: (bi, qi, h)),
        out_shape=jax.ShapeDtypeStruct((b, seq, ATTN_WIDTH), BF16),
        scratch_shapes=_head_scratch(heads, seq, tk, 2 * tq),
        compiler_params=_params("arbitrary", "arbitrary", "arbitrary"),
        name="diff_attn",
    )(lq1, lk1, lq2, lk2, subln_g, cos_t, sin_t, cos_t, sin_t, _diff_causal_bias(tq, tk),
      proj3, proj3, proj3)


def _block_of(pos):
    return jnp.right_shift(pos, MOBA_BLOCK.bit_length() - 1)


def _moba_diag_bias(tq):
    r = lax.broadcasted_iota(jnp.int32, (tq, tq), 0)
    c = lax.broadcasted_iota(jnp.int32, (tq, tq), 1)
    r_blk, c_blk = _block_of(r), _block_of(c)
    keep = ((r <= c) & (r_blk == c_blk)) | (r_blk < c_blk)
    return jnp.where(keep, 0.0, NEG_INF).astype(F32)


def _moba_body(cq_ref, sq_ref, ck_ref, sk_ref, bias_ref, q_ref, k_ref, v_ref, o_ref,
               qt_ref, kr_ref, vt_ref, s_ref, mb_ref, m_ref, l_ref, acc_ref, km_ref, mem_ref,
               *, heads, seq, tq, tk):
    qi = pl.program_id(2)
    blk = MOBA_BLOCK
    hd = MOBA_HEAD_DIM
    nb = seq // blk
    per = tk // blk
    head_lanes = [slice(hh * HEAD_LANES, (hh + 1) * HEAD_LANES) for hh in range(heads)]

    @pl.when(qi == 0)
    def _():
        n_i = lax.broadcasted_iota(jnp.int32, (nb, seq), 0)
        s_i = lax.broadcasted_iota(jnp.int32, (nb, seq), 1)
        ind = jnp.where(_block_of(s_i) == n_i, 1.0 / blk, 0.0).astype(BF16)
        for hh in range(heads):
            _prepare_keys_values(k_ref, v_ref, ck_ref, sk_ref, kr_ref.at[hh], vt_ref.at[hh],
                                 head_lanes[hh], seq, tk, hd)
            kmean = jnp.dot(ind, kr_ref[hh], preferred_element_type=F32)
            hi = kmean.astype(BF16)
            km_ref[hh, :nb, :] = hi
            km_ref[hh, nb:, :] = (kmean - hi.astype(F32)).astype(BF16)

    n_i = lax.broadcasted_iota(jnp.int32, (nb, tq), 0)
    c_i = lax.broadcasted_iota(jnp.int32, (nb, tq), 1)
    is_past = n_i < _block_of(qi * tq + c_i)
    for hh in range(heads):
        qt = _rope_t(q_ref[:, head_lanes[hh]].astype(F32).T, cq_ref[...], sq_ref[...], hd)
        qt_ref[hh] = (qt * (hd ** -0.5 * LOG2E)).astype(BF16)

        g2 = jnp.dot(km_ref[hh], qt_ref[hh], preferred_element_type=F32)
        gm = jnp.where(is_past, g2[:nb, :] + g2[nb:, :], NEG_INF)
        member = jnp.zeros((nb, tq), F32)
        for _ in range(min(MOBA_TOPK, nb)):
            mx = jnp.max(gm, axis=0, keepdims=True)
            first = jnp.min(jnp.where(gm == mx, n_i, nb), axis=0, keepdims=True)
            sel = n_i == first
            member = jnp.where(sel & is_past, 1.0, member)
            gm = jnp.where(sel, REMOVED, gm)
        for n in range(nb):
            mem_ref[hh, n] = member[n:n + 1, :]

    def make_stream(hh):
        def scores(kb):
            k_blk = kr_ref[hh, pl.ds(pl.multiple_of(kb * tk, tk), tk), :]
            return jnp.dot(k_blk, qt_ref[hh], preferred_element_type=F32)

        def past_mask(kb, sc):
            return jnp.concatenate(
                [jnp.where(mem_ref[hh, kb * per + t] > 0.0, sc[t * blk:(t + 1) * blk, :], NEG_INF)
                 for t in range(per)], axis=0)

        def diag_mask(sc):
            sc = sc + bias_ref[...]
            c_blk = _block_of(lax.broadcasted_iota(jnp.int32, (1, tq), 1))
            parts = []
            for t in range(per - 1):
                keep = (c_blk <= t) | (mem_ref[hh, qi * per + t] > 0.0)
                parts.append(sc[t * blk:(t + 1) * blk, :] + jnp.where(keep, 0.0, NEG_INF))
            return jnp.concatenate(parts + [sc[(per - 1) * blk:, :]], axis=0)

        return _Stream(scores, past_mask, diag_mask, vt_ref.at[hh], s_ref.at[hh],
                       mb_ref.at[hh], m_ref.at[hh], l_ref.at[hh], acc_ref.at[hh])

    _attend(qi, [make_stream(hh) for hh in range(heads)])

    for hh in range(heads):
        o_ref[:, head_lanes[hh]] = (acc_ref[hh] * (1.0 / l_ref[hh])).T.astype(BF16)


def _moba_attention(proj3, cos_t, sin_t):
    b, seq, _ = proj3.shape
    tq = tk = min(MOBA_TQ, seq)
    heads = ATTN_HEADS_PER_STEP
    bw = heads * HEAD_LANES
    nb = seq // MOBA_BLOCK
    half = cos_t.shape[0]
    qcol = 3 * ATTN_WIDTH // bw
    kcol = 4 * ATTN_WIDTH // bw
    vcol = 5 * ATTN_WIDTH // bw
    tab_q = pl.BlockSpec((half, tq), lambda bi, h, qi: (0, qi))
    tab_k = pl.BlockSpec((half, seq), lambda bi, h, qi: (0, 0))
    return pl.pallas_call(
        functools.partial(_moba_body, heads=heads, seq=seq, tq=tq, tk=tk),
        grid=(b, MOBA_HEADS // heads, seq // tq),
        in_specs=[
            tab_q, tab_q, tab_k, tab_k,
            pl.BlockSpec((tq, tq), lambda bi, h, qi: (0, 0), **ONCE),
            pl.BlockSpec((None, tq, bw), lambda bi, h, qi: (bi, qi, qcol + h)),
            pl.BlockSpec((None, seq, bw), lambda bi, h, qi: (bi, 0, kcol + h)),
            pl.BlockSpec((None, seq, bw), lambda bi, h, qi: (bi, 0, vcol + h)),
        ],
        out_specs=pl.BlockSpec((None, tq, bw), lambda bi, h, qi: (bi, qi, h)),
        out_shape=jax.ShapeDtypeStruct((b, seq, ATTN_WIDTH), BF16),
        scratch_shapes=_head_scratch(heads, seq, tk, tq) + [
            pltpu.VMEM((heads, 2 * nb, MOBA_HEAD_DIM), BF16),
            pltpu.VMEM((heads, nb, 1, tq), F32),
        ],
        compiler_params=_params("arbitrary", "arbitrary", "arbitrary"),
        name="moba_attn",
    )(cos_t, sin_t, cos_t, sin_t, _moba_diag_bias(tq), proj3, proj3, proj3)


def _merge_body(ya_ref, yb_ref, ga_ref, gb_ref, x_ref, wa_ref, wb_ref, wo_ref, g_ref, o_ref):
    ta = jnp.dot(ya_ref[...], wa_ref[...], preferred_element_type=F32)
    tb = jnp.dot(yb_ref[...], wb_ref[...], preferred_element_type=F32)
    gate_a = jax.nn.sigmoid(ga_ref[...].astype(F32))
    gate_b = jax.nn.sigmoid(gb_ref[...].astype(F32))
    merged = (gate_a * ta + gate_b * tb).astype(BF16)
    y = jnp.dot(merged, wo_ref[...], preferred_element_type=F32)
    o_ref[...] = x_ref[...] + _rms(y, g_ref[...], NORM_EPS)


def _merge(ya, yb, proj, x, w_a, w_b, w_o, g, layer):
    m, d = x.shape
    tm = min(MERGE_TM, m)
    gate_col = 6 * ATTN_WIDTH // d
    return pl.pallas_call(
        _merge_body,
        grid=(m // tm,),
        in_specs=[
            pl.BlockSpec((tm, ATTN_WIDTH), lambda i: (i, 0)),
            pl.BlockSpec((tm, ATTN_WIDTH), lambda i: (i, 0)),
            pl.BlockSpec((tm, d), lambda i: (i, gate_col)),
            pl.BlockSpec((tm, d), lambda i: (i, gate_col + 1)),
            pl.BlockSpec((tm, d), lambda i: (i, 0)),
            pl.BlockSpec((None, ATTN_WIDTH, d), lambda i: (layer, 0, 0), **ONCE),
            pl.BlockSpec((None, ATTN_WIDTH, d), lambda i: (layer, 0, 0), **ONCE),
            pl.BlockSpec((None, d, d), lambda i: (layer, 0, 0), **ONCE),
            pl.BlockSpec((None, 1, d), lambda i: (layer, 0, 0)),
        ],
        out_specs=pl.BlockSpec((tm, d), lambda i: (i, 0)),
        out_shape=jax.ShapeDtypeStruct((m, d), F32),
        compiler_params=_params("arbitrary"),
        name="merge",
    )(ya, yb, proj, proj, x, w_a, w_b, w_o, g)


def _trunk(x, ffn1_pre_g, ffn1_w_gu, ffn1_w_down, ffn1_post_g, mix_pre_g, w_in,
           diff_lq1, diff_lk1, diff_lq2, diff_lk2, diff_subln_g,
           w_branch_diff, w_branch_moba, w_out, mix_post_g,
           ffn2_pre_g, ffn2_w_gu, ffn2_w_down, ffn2_post_g):
    b, seq, d = x.shape
    depth = w_in.shape[0]
    assert (6 * ATTN_WIDTH) % d == 0 and seq % ATTN_TK == 0
    row = lambda p: p[:, None, :]
    bf = lambda w: w.astype(BF16)
    cos_d, sin_d = _rope_tables_t(seq, DIFF_HEAD_DIM)
    cos_m, sin_m = _rope_tables_t(seq, MOBA_HEAD_DIM)
    w1gu, w1d, w2gu, w2d = bf(ffn1_w_gu), bf(ffn1_w_down), bf(ffn2_w_gu), bf(ffn2_w_down)
    win, wa, wb, wo = bf(w_in), bf(w_branch_diff), bf(w_branch_moba), bf(w_out)
    xt = x.reshape(b * seq, d)
    for l in range(depth):
        lambda_init = 0.8 - 0.6 * math.exp(-0.3 * l)
        xt = _ffn(xt, row(ffn1_pre_g), w1gu, w1d, row(ffn1_post_g), l)
        proj = _inproj(xt, row(mix_pre_g), win, l)
        proj3 = proj.reshape(b, seq, proj.shape[1])
        ya = _diff_attention(proj3, cos_d, sin_d, row(diff_lq1), row(diff_lk1), row(diff_lq2),
                             row(diff_lk2), row(diff_subln_g), l, lambda_init)
        yb = _moba_attention(proj3, cos_m, sin_m)
        xt = _merge(ya.reshape(b * seq, ATTN_WIDTH), yb.reshape(b * seq, ATTN_WIDTH), proj, xt,
                    wa, wb, wo, row(mix_post_g), l)
        xt = _ffn(xt, row(ffn2_pre_g), w2gu, w2d, row(ffn2_post_g), l)
    return xt.reshape(b, seq, d)


def kernel(x, ffn1_pre_g, ffn1_w_gu, ffn1_w_down, ffn1_post_g, mix_pre_g, w_in, diff_lq1, diff_lk1, diff_lq2, diff_lk2, diff_subln_g, w_branch_diff, w_branch_moba, w_out, mix_post_g, ffn2_pre_g, ffn2_w_gu, ffn2_w_down, ffn2_post_g):
    return _trunk(x, ffn1_pre_g, ffn1_w_gu, ffn1_w_down, ffn1_post_g, mix_pre_g, w_in,
                  diff_lq1, diff_lk1, diff_lq2, diff_lk2, diff_subln_g,
                  w_branch_diff, w_branch_moba, w_out, mix_post_g,
                  ffn2_pre_g, ffn2_w_gu, ffn2_w_down, ffn2_post_g)
```

```python
import functools
import math
from typing import Any, Callable, NamedTuple

import jax
import jax.numpy as jnp
from jax import lax
from jax.experimental import pallas as pl
from jax.experimental.pallas import tpu as pltpu

F32 = jnp.float32
BF16 = jnp.bfloat16

DIFF_HEADS = 8
DIFF_HEAD_DIM = 64
MOBA_HEADS = 8
MOBA_HEAD_DIM = 128
MOBA_BLOCK = 256
MOBA_TOPK = 3
ATTN_WIDTH = 1024
HEAD_LANES = 128
ROPE_THETA = 500000.0
ROT_FRAC_DIV = 4
NORM_EPS = 1e-6
SUBLN_EPS = 1e-5
NEG_INF = -1e30
REMOVED = -3e38
LOG2E = math.log2(math.e)

SUBLANES = 8
BF16_SUBLANES = 16
ACC_ROWS = HEAD_LANES + BF16_SUBLANES
VMEM_LIMIT_BYTES = 56 * 1024 * 1024

FFN_TM = 1024
FFN_TF = 512
INPROJ_TM = 1024
INPROJ_TN = 1024
MERGE_TM = 512
ATTN_TK = 512
ATTN_COLS = 512
ATTN_HEADS_PER_STEP = 4
DIFF_TQ = ATTN_COLS // 2
MOBA_TQ = ATTN_COLS

ONCE = dict(pipeline_mode=pl.Buffered(1))


def _params(*sem):
    return pltpu.CompilerParams(dimension_semantics=sem, vmem_limit_bytes=VMEM_LIMIT_BYTES)


def _rms(x, g, eps):
    return x * lax.rsqrt(jnp.mean(x * x, axis=-1, keepdims=True) + eps) * g


def _ffn_body(x_ref, pre_g_ref, wg_ref, wu_ref, wd_ref, post_g_ref, o_ref, h_ref, a_ref, *, nf):
    j = pl.program_id(1)

    def up(slot):
        h = h_ref[...]
        g = jnp.dot(h, wg_ref[...], preferred_element_type=F32)
        u = jnp.dot(h, wu_ref[...], preferred_element_type=F32)
        a_ref[slot] = (g * jax.nn.sigmoid(g) * u).astype(BF16)

    def down(slot):
        o_ref[...] += jnp.dot(a_ref[slot], wd_ref[...], preferred_element_type=F32)

    @pl.when(j == 0)
    def _():
        h_ref[...] = _rms(x_ref[...], pre_g_ref[...], NORM_EPS).astype(BF16)
        o_ref[...] = jnp.zeros_like(o_ref)
        up(0)

    middle = (j > 0) & (j < nf)

    @pl.when(middle & (j % 2 == 1))
    def _():
        up(1)
        down(0)

    @pl.when(middle & (j % 2 == 0))
    def _():
        up(0)
        down(1)

    @pl.when(j == nf)
    def _():
        down((nf - 1) % 2)
        o_ref[...] = x_ref[...] + 0.5 * _rms(o_ref[...], post_g_ref[...], NORM_EPS)


def _ffn(x, pre_g, w_gu, w_down, post_g, layer):
    m, d = x.shape
    f = w_down.shape[1]
    tm, tf = min(FFN_TM, m), min(FFN_TF, f)
    nf = f // tf
    up_blk = lambda j: jnp.minimum(j, nf - 1)
    down_blk = lambda j: jnp.maximum(j - 1, 0)
    return pl.pallas_call(
        functools.partial(_ffn_body, nf=nf),
        grid=(m // tm, nf + 1),
        in_specs=[
            pl.BlockSpec((tm, d), lambda i, j: (i, 0)),
            pl.BlockSpec((None, 1, d), lambda i, j: (layer, 0, 0)),
            pl.BlockSpec((None, d, tf), lambda i, j: (layer, 0, up_blk(j))),
            pl.BlockSpec((None, d, tf), lambda i, j: (layer, 0, up_blk(j) + nf)),
            pl.BlockSpec((None, tf, d), lambda i, j: (layer, down_blk(j), 0)),
            pl.BlockSpec((None, 1, d), lambda i, j: (layer, 0, 0)),
        ],
        out_specs=pl.BlockSpec((tm, d), lambda i, j: (i, 0)),
        out_shape=jax.ShapeDtypeStruct((m, d), F32),
        scratch_shapes=[
            pltpu.VMEM((tm, d), BF16),
            pltpu.VMEM((2, tm, tf), BF16),
        ],
        compiler_params=_params("arbitrary", "arbitrary"),
        name="ffn",
    )(x, pre_g, w_gu, w_gu, w_down, post_g)


def _inproj_body(x_ref, g_ref, w_ref, o_ref, h_ref):
    @pl.when(pl.program_id(1) == 0)
    def _():
        h_ref[...] = _rms(x_ref[...], g_ref[...], NORM_EPS).astype(BF16)

    o_ref[...] = jnp.dot(h_ref[...], w_ref[...], preferred_element_type=F32).astype(BF16)


def _inproj(x, g, w_in, layer):
    m, d = x.shape
    width = w_in.shape[2]
    tm = min(INPROJ_TM, m)
    tn = min(INPROJ_TN, ATTN_WIDTH)
    return pl.pallas_call(
        _inproj_body,
        grid=(m // tm, width // tn),
        in_specs=[
            pl.BlockSpec((tm, d), lambda i, j: (i, 0)),
            pl.BlockSpec((None, 1, d), lambda i, j: (layer, 0, 0)),
            pl.BlockSpec((None, d, tn), lambda i, j: (layer, 0, j)),
        ],
        out_specs=pl.BlockSpec((tm, tn), lambda i, j: (i, j)),
        out_shape=jax.ShapeDtypeStruct((m, width), BF16),
        scratch_shapes=[pltpu.VMEM((tm, d), BF16)],
        compiler_params=_params("arbitrary", "arbitrary"),
        name="inproj",
    )(x, g, w_in)


def _rope_tables_t(seq, head_dim):
    rot = head_dim // ROT_FRAC_DIV
    inv = jnp.power(ROPE_THETA, -jnp.arange(0, rot, 2, dtype=F32) / rot)
    ang = inv[:, None] * jnp.arange(seq, dtype=F32)[None, :]
    return jnp.cos(ang), jnp.sin(ang)


def _rope_t(xt, cos, sin, head_dim):
    half = cos.shape[0]
    assert half % SUBLANES == 0
    parts = []
    for base in range(0, xt.shape[0], head_dim):
        x1 = xt[base:base + half]
        x2 = xt[base + half:base + 2 * half]
        parts += [x1 * cos - x2 * sin, x2 * cos + x1 * sin, xt[base + 2 * half:base + head_dim]]
    return jnp.concatenate(parts, axis=0)


class _Stream(NamedTuple):
    scores: Callable[[Any], Any]
    diag_bias: Callable[[], Any]
    vt: Any
    s: Any
    mb: Any
    m: Any
    acc: Any


def _head_scratch(heads, seq, tk, cols, key_lanes):
    return [
        pltpu.VMEM((heads, key_lanes, cols), BF16),
        pltpu.VMEM((heads, seq, key_lanes), BF16),
        pltpu.VMEM((heads, seq // tk, ACC_ROWS, tk), BF16),
        pltpu.VMEM((heads, 2, tk, cols), F32),
        pltpu.VMEM((heads, 2, 1, cols), F32),
        pltpu.VMEM((heads, 1, cols), F32),
        pltpu.VMEM((heads, ACC_ROWS, cols), F32),
    ]


def _prepare_keys_values(k_ref, v_ref, cos_ref, sin_ref, kr_ref, vt_ref, lanes, seq, tk, head_dim):
    extra = lax.broadcasted_iota(jnp.int32, (ACC_ROWS - HEAD_LANES, tk), 0)
    ones_row = jnp.where(extra == 0, 1.0, 0.0).astype(BF16)
    for c in range(seq // tk):
        rows = slice(c * tk, (c + 1) * tk)
        kt = _rope_t(k_ref[rows, lanes].astype(F32).T, cos_ref[:, rows], sin_ref[:, rows], head_dim)
        kr_ref[rows, :HEAD_LANES] = kt.T.astype(BF16)
        vt_ref[c, :HEAD_LANES, :] = v_ref[rows, lanes].astype(F32).T.astype(BF16)
        vt_ref[c, HEAD_LANES:, :] = ones_row


def _attend(n_past, max_past, streams, prologue, epilogue):
    def produce(kb, slot, past):
        for st in streams:
            sc = st.scores(kb)
            if past:
                st.mb[slot] = jnp.max(sc, axis=0, keepdims=True)
            st.s[slot] = sc

    def consume(kb, slot, diag):
        for st in streams:
            sc = st.s[slot]
            vt_blk = st.vt[kb]
            if diag:
                sc = sc + st.diag_bias()
                m_new = jnp.max(sc, axis=0, keepdims=True)
                p = jnp.exp2(sc - m_new).astype(BF16)
                st.acc[...] = jnp.dot(vt_blk, p, preferred_element_type=F32)
            else:
                m_old = st.m[...]
                m_new = jnp.maximum(m_old, st.mb[slot])
                alpha = jnp.exp2(m_old - m_new)
                p = jnp.exp2(sc - m_new).astype(BF16)
                st.acc[...] = alpha * st.acc[...] + jnp.dot(vt_blk, p, preferred_element_type=F32)
            st.m[...] = m_new

    def run(n):
        prologue()
        produce(n, 0, False)
        if n > 0:
            produce(0, 1, True)
        consume(n, 0, True)
        for c in range(n):
            if c + 1 < n:
                produce(c + 1, c % 2, True)
            consume(c, (c + 1) % 2, False)
        epilogue()

    for n in range(max_past + 1):
        pl.when(n_past == n)(functools.partial(run, n))


def _diff_causal_bias(tq, tk):
    p = lax.broadcasted_iota(jnp.int32, (tk // tq, tk, 2 * tq), 0)
    r = lax.broadcasted_iota(jnp.int32, (tk // tq, tk, 2 * tq), 1)
    c = lax.broadcasted_iota(jnp.int32, (tk // tq, tk, 2 * tq), 2) % tq
    return jnp.where(r <= p * tq + c, 0.0, NEG_INF).astype(F32)


def _diff_body(lq1_ref, lk1_ref, lq2_ref, lk2_ref, sg_ref, cq_ref, sq_ref, ck_ref, sk_ref, bias_ref,
               q_ref, k_ref, v_ref, o_ref,
               qt_ref, kr_ref, vt_ref, s_ref, mb_ref, m_ref, acc_ref,
               *, heads, seq, tq, tk, lambda_init):
    qi = pl.program_id(2)
    d = DIFF_HEAD_DIM
    head_lanes = [slice(hh * HEAD_LANES, (hh + 1) * HEAD_LANES) for hh in range(heads)]

    @pl.when(qi == 0)
    def _():
        for hh in range(heads):
            _prepare_keys_values(k_ref, v_ref, ck_ref, sk_ref, kr_ref.at[hh], vt_ref.at[hh],
                                 head_lanes[hh], seq, tk, d)

    def prologue():
        row = lax.broadcasted_iota(jnp.int32, (2 * d, tq), 0)
        for hh in range(heads):
            qt = _rope_t(q_ref[:, head_lanes[hh]].astype(F32).T, cq_ref[...], sq_ref[...], d)
            qt = qt * (d ** -0.5 * LOG2E)
            qt_ref[hh, :, :tq] = jnp.where(row < d, qt, 0.0).astype(BF16)
            qt_ref[hh, :, tq:] = jnp.where(row >= d, qt, 0.0).astype(BF16)

    def make_stream(hh):
        def scores(kb):
            k_blk = kr_ref[hh, pl.ds(kb * tk, tk), :]
            return jnp.dot(k_blk, qt_ref[hh], preferred_element_type=F32)

        return _Stream(scores, lambda: bias_ref[qi % (tk // tq)], vt_ref.at[hh], s_ref.at[hh],
                       mb_ref.at[hh], m_ref.at[hh], acc_ref.at[hh])

    def epilogue():
        lam = (jnp.exp(jnp.sum(lq1_ref[...] * lk1_ref[...], axis=-1, keepdims=True))
               - jnp.exp(jnp.sum(lq2_ref[...] * lk2_ref[...], axis=-1, keepdims=True))
               + lambda_init)
        for hh in range(heads):
            denom = acc_ref[hh, HEAD_LANES:HEAD_LANES + 1, :]
            acc = acc_ref[hh, :HEAD_LANES, :] * (1.0 / denom)
            o = (acc[:, :tq] - lam * acc[:, tq:]).T
            o_ref[:, head_lanes[hh]] = (
                _rms(o, sg_ref[...], SUBLN_EPS) * (1.0 - lambda_init)).astype(BF16)

    _attend((qi * tq) // tk, (seq - tq) // tk, [make_stream(hh) for hh in range(heads)],
            prologue, epilogue)


def _diff_attention(proj3, cos_t, sin_t, lq1, lk1, lq2, lk2, subln_g, layer, lambda_init):
    b, seq, _ = proj3.shape
    tq, tk = min(DIFF_TQ, seq), min(ATTN_TK, seq)
    heads = ATTN_HEADS_PER_STEP
    bw = heads * HEAD_LANES
    half = cos_t.shape[0]
    kcol = ATTN_WIDTH // bw
    vcol = 2 * ATTN_WIDTH // bw
    lspec = pl.BlockSpec((None, 1, DIFF_HEAD_DIM), lambda bi, h, qi: (layer, 0, 0))
    tab_q = pl.BlockSpec((half, tq), lambda bi, h, qi: (0, qi))
    tab_k = pl.BlockSpec((half, seq), lambda bi, h, qi: (0, 0))
    return pl.pallas_call(
        functools.partial(_diff_body, heads=heads, seq=seq, tq=tq, tk=tk, lambda_init=lambda_init),
        grid=(b, DIFF_HEADS // heads, seq // tq),
        in_specs=[
            lspec, lspec, lspec, lspec,
            pl.BlockSpec((None, 1, HEAD_LANES), lambda bi, h, qi: (layer, 0, 0)),
            tab_q, tab_q, tab_k, tab_k,
            pl.BlockSpec((tk // tq, tk, 2 * tq), lambda bi, h, qi: (0, 0, 0), **ONCE),
            pl.BlockSpec((None, tq, bw), lambda bi, h, qi: (bi, qi, h)),
            pl.BlockSpec((None, seq, bw), lambda bi, h, qi: (bi, 0, kcol + h)),
            pl.BlockSpec((None, seq, bw), lambda bi, h, qi: (bi, 0, vcol + h)),
        ],
        out_specs=pl.BlockSpec((None, tq, bw), lambda bi, h, qi: (bi, qi, h)),
        out_shape=jax.ShapeDtypeStruct((b, seq, ATTN_WIDTH), BF16),
        scratch_shapes=_head_scratch(heads, seq, tk, 2 * tq, HEAD_LANES),
        compiler_params=_params("arbitrary", "arbitrary", "arbitrary"),
        name="diff_attn",
    )(lq1, lk1, lq2, lk2, subln_g, cos_t, sin_t, cos_t, sin_t, _diff_causal_bias(tq, tk),
      proj3, proj3, proj3)


def _block_of(pos):
    return jnp.right_shift(pos, MOBA_BLOCK.bit_length() - 1)


def _moba_diag_bias(tq):
    r = lax.broadcasted_iota(jnp.int32, (tq, tq), 0)
    c = lax.broadcasted_iota(jnp.int32, (tq, tq), 1)
    r_blk, c_blk = _block_of(r), _block_of(c)
    keep = ((r <= c) & (r_blk == c_blk)) | (r_blk < c_blk)
    return jnp.where(keep, 0.0, NEG_INF).astype(F32)


def _moba_body(cq_ref, sq_ref, ck_ref, sk_ref, bias_ref, q_ref, k_ref, v_ref, o_ref,
               qt_ref, kr_ref, vt_ref, s_ref, mb_ref, m_ref, acc_ref, km_ref,
               *, heads, seq, tq, tk):
    qi = pl.program_id(2)
    blk = MOBA_BLOCK
    hd = MOBA_HEAD_DIM
    nb = seq // blk
    assert nb <= HEAD_LANES
    head_lanes = [slice(hh * HEAD_LANES, (hh + 1) * HEAD_LANES) for hh in range(heads)]

    @pl.when(qi == 0)
    def _():
        n_i = lax.broadcasted_iota(jnp.int32, (nb, seq), 0)
        s_i = lax.broadcasted_iota(jnp.int32, (nb, seq), 1)
        ind = jnp.where(_block_of(s_i) == n_i, 1.0 / blk, 0.0).astype(BF16)
        r_i = lax.broadcasted_iota(jnp.int32, (tk, HEAD_LANES), 0)
        l_i = lax.broadcasted_iota(jnp.int32, (tk, HEAD_LANES), 1)
        for hh in range(heads):
            _prepare_keys_values(k_ref, v_ref, ck_ref, sk_ref, kr_ref.at[hh], vt_ref.at[hh],
                                 head_lanes[hh], seq, tk, hd)
            for c in range(seq // tk):
                onehot = jnp.where(_block_of(c * tk + r_i) == l_i, 1.0, 0.0).astype(BF16)
                kr_ref[hh, c * tk:(c + 1) * tk, HEAD_LANES:] = onehot
            kmean = jnp.dot(ind, kr_ref[hh, :, :HEAD_LANES], preferred_element_type=F32)
            hi = kmean.astype(BF16)
            km_ref[hh, :nb, :] = hi
            km_ref[hh, nb:, :] = (kmean - hi.astype(F32)).astype(BF16)

    def prologue():
        n_i = lax.broadcasted_iota(jnp.int32, (nb, tq), 0)
        c_i = lax.broadcasted_iota(jnp.int32, (nb, tq), 1)
        is_past = n_i < _block_of(qi * tq + c_i)
        for hh in range(heads):
            qt = _rope_t(q_ref[:, head_lanes[hh]].astype(F32).T, cq_ref[...], sq_ref[...], hd)
            qt = (qt * (hd ** -0.5 * LOG2E)).astype(BF16)
            qt_ref[hh, :HEAD_LANES, :] = qt

            g2 = jnp.dot(km_ref[hh], qt, preferred_element_type=F32)
            gm = jnp.where(is_past, g2[:nb, :] + g2[nb:, :], NEG_INF)
            gate = jnp.where(is_past, NEG_INF, 0.0)
            for _ in range(min(MOBA_TOPK, nb)):
                mx = jnp.max(gm, axis=0, keepdims=True)
                first = jnp.min(jnp.where(gm == mx, n_i, nb), axis=0, keepdims=True)
                sel = n_i == first
                gate = jnp.where(sel, 0.0, gate)
                gm = jnp.where(sel, REMOVED, gm)
            qt_ref[hh, HEAD_LANES:, :] = jnp.concatenate(
                [gate, jnp.zeros((HEAD_LANES - nb, tq), F32)], axis=0).astype(BF16)

    def make_stream(hh):
        def scores(kb):
            k_blk = kr_ref[hh, pl.ds(kb * tk, tk), :]
            return jnp.dot(k_blk, qt_ref[hh], preferred_element_type=F32)

        return _Stream(scores, lambda: bias_ref[...], vt_ref.at[hh], s_ref.at[hh],
                       mb_ref.at[hh], m_ref.at[hh], acc_ref.at[hh])

    def epilogue():
        for hh in range(heads):
            denom = acc_ref[hh, HEAD_LANES:HEAD_LANES + 1, :]
            o = acc_ref[hh, :HEAD_LANES, :] * (1.0 / denom)
            o_ref[:, head_lanes[hh]] = o.T.astype(BF16)

    _attend(qi, seq // tq - 1, [make_stream(hh) for hh in range(heads)], prologue, epilogue)


def _moba_attention(proj3, cos_t, sin_t):
    b, seq, _ = proj3.shape
    tq = tk = min(MOBA_TQ, seq)
    heads = ATTN_HEADS_PER_STEP
    bw = heads * HEAD_LANES
    nb = seq // MOBA_BLOCK
    half = cos_t.shape[0]
    qcol = 3 * ATTN_WIDTH // bw
    kcol = 4 * ATTN_WIDTH // bw
    vcol = 5 * ATTN_WIDTH // bw
    tab_q = pl.BlockSpec((half, tq), lambda bi, h, qi: (0, qi))
    tab_k = pl.BlockSpec((half, seq), lambda bi, h, qi: (0, 0))
    return pl.pallas_call(
        functools.partial(_moba_body, heads=heads, seq=seq, tq=tq, tk=tk),
        grid=(b, MOBA_HEADS // heads, seq // tq),
        in_specs=[
            tab_q, tab_q, tab_k, tab_k,
            pl.BlockSpec((tq, tq), lambda bi, h, qi: (0, 0), **ONCE),
            pl.BlockSpec((None, tq, bw), lambda bi, h, qi: (bi, qi, qcol + h)),
            pl.BlockSpec((None, seq, bw), lambda bi, h, qi: (bi, 0, kcol + h)),
            pl.BlockSpec((None, seq, bw), lambda bi, h, qi: (bi, 0, vcol + h)),
        ],
        out_specs=pl.BlockSpec((None, tq, bw), lambda bi, h, qi: (bi, qi, h)),
        out_shape=jax.ShapeDtypeStruct((b, seq, ATTN_WIDTH), BF16),
        scratch_shapes=_head_scratch(heads, seq, tk, tq, 2 * HEAD_LANES) + [
            pltpu.VMEM((heads, 2 * nb, MOBA_HEAD_DIM), BF16),
        ],
        compiler_params=_params("arbitrary", "arbitrary", "arbitrary"),
        name="moba_attn",
    )(cos_t, sin_t, cos_t, sin_t, _moba_diag_bias(tq), proj3, proj3, proj3)


def _merge_body(ya_ref, yb_ref, ga_ref, gb_ref, x_ref, wa_ref, wb_ref, wo_ref, g_ref, o_ref):
    ta = jnp.dot(ya_ref[...], wa_ref[...], preferred_element_type=F32)
    tb = jnp.dot(yb_ref[...], wb_ref[...], preferred_element_type=F32)
    gate_a = jax.nn.sigmoid(ga_ref[...].astype(F32))
    gate_b = jax.nn.sigmoid(gb_ref[...].astype(F32))
    merged = (gate_a * ta + gate_b * tb).astype(BF16)
    y = jnp.dot(merged, wo_ref[...], preferred_element_type=F32)
    o_ref[...] = x_ref[...] + _rms(y, g_ref[...], NORM_EPS)


def _merge(ya, yb, proj, x, w_a, w_b, w_o, g, layer):
    m, d = x.shape
    tm = min(MERGE_TM, m)
    gate_col = 6 * ATTN_WIDTH // d
    return pl.pallas_call(
        _merge_body,
        grid=(m // tm,),
        in_specs=[
            pl.BlockSpec((tm, ATTN_WIDTH), lambda i: (i, 0)),
            pl.BlockSpec((tm, ATTN_WIDTH), lambda i: (i, 0)),
            pl.BlockSpec((tm, d), lambda i: (i, gate_col)),
            pl.BlockSpec((tm, d), lambda i: (i, gate_col + 1)),
            pl.BlockSpec((tm, d), lambda i: (i, 0)),
            pl.BlockSpec((None, ATTN_WIDTH, d), lambda i: (layer, 0, 0), **ONCE),
            pl.BlockSpec((None, ATTN_WIDTH, d), lambda i: (layer, 0, 0), **ONCE),
            pl.BlockSpec((None, d, d), lambda i: (layer, 0, 0), **ONCE),
            pl.BlockSpec((None, 1, d), lambda i: (layer, 0, 0)),
        ],
        out_specs=pl.BlockSpec((tm, d), lambda i: (i, 0)),
        out_shape=jax.ShapeDtypeStruct((m, d), F32),
        compiler_params=_params("arbitrary"),
        name="merge",
    )(ya, yb, proj, proj, x, w_a, w_b, w_o, g)


def _trunk(x, ffn1_pre_g, ffn1_w_gu, ffn1_w_down, ffn1_post_g, mix_pre_g, w_in,
           diff_lq1, diff_lk1, diff_lq2, diff_lk2, diff_subln_g,
           w_branch_diff, w_branch_moba, w_out, mix_post_g,
           ffn2_pre_g, ffn2_w_gu, ffn2_w_down, ffn2_post_g):
    b, seq, d = x.shape
    depth = w_in.shape[0]
    assert (6 * ATTN_WIDTH) % d == 0 and seq % ATTN_TK == 0
    row = lambda p: p[:, None, :]
    bf = lambda w: w.astype(BF16)
    cos_d, sin_d = _rope_tables_t(seq, DIFF_HEAD_DIM)
    cos_m, sin_m = _rope_tables_t(seq, MOBA_HEAD_DIM)
    w1gu, w1d, w2gu, w2d = bf(ffn1_w_gu), bf(ffn1_w_down), bf(ffn2_w_gu), bf(ffn2_w_down)
    win, wa, wb, wo = bf(w_in), bf(w_branch_diff), bf(w_branch_moba), bf(w_out)
    xt = x.reshape(b * seq, d)
    for l in range(depth):
        lambda_init = 0.8 - 0.6 * math.exp(-0.3 * l)
        xt = _ffn(xt, row(ffn1_pre_g), w1gu, w1d, row(ffn1_post_g), l)
        proj = _inproj(xt, row(mix_pre_g), win, l)
        proj3 = proj.reshape(b, seq, proj.shape[1])
        ya = _diff_attention(proj3, cos_d, sin_d, row(diff_lq1), row(diff_lk1), row(diff_lq2),
                             row(diff_lk2), row(diff_subln_g), l, lambda_init)
        yb = _moba_attention(proj3, cos_m, sin_m)
        xt = _merge(ya.reshape(b * seq, ATTN_WIDTH), yb.reshape(b * seq, ATTN_WIDTH), proj, xt,
                    wa, wb, wo, row(mix_post_g), l)
        xt = _ffn(xt, row(ffn2_pre_g), w2gu, w2d, row(ffn2_post_g), l)
    return xt.reshape(b, seq, d)


def kernel(x, ffn1_pre_g, ffn1_w_gu, ffn1_w_down, ffn1_post_g, mix_pre_g, w_in, diff_lq1, diff_lk1, diff_lq2, diff_lk2, diff_subln_g, w_branch_diff, w_branch_moba, w_out, mix_post_g, ffn2_pre_g, ffn2_w_gu, ffn2_w_down, ffn2_post_g):
    return _trunk(x, ffn1_pre_g, ffn1_w_gu, ffn1_w_down, ffn1_post_g, mix_pre_g, w_in,
                  diff_lq1, diff_lk1, diff_lq2, diff_lk2, diff_subln_g,
                  w_branch_diff, w_branch_moba, w_out, mix_post_g,
                  ffn2_pre_g, ffn2_w_gu, ffn2_w_down, ffn2_post_g)
```

```python
import functools
import math
from typing import Any, Callable, NamedTuple

import jax
import jax.numpy as jnp
from jax import lax
from jax.experimental import pallas as pl
from jax.experimental.pallas import tpu as pltpu

F32 = jnp.float32
BF16 = jnp.bfloat16

DIFF_HEADS = 8
DIFF_HEAD_DIM = 64
MOBA_HEADS = 8
MOBA_HEAD_DIM = 128
MOBA_BLOCK = 256
MOBA_TOPK = 3
ATTN_WIDTH = 1024
HEAD_LANES = 128
ROPE_THETA = 500000.0
ROT_FRAC_DIV = 4
NORM_EPS = 1e-6
SUBLN_EPS = 1e-5
NEG_INF = -1e30
REMOVED = -3e38
LOG2E = math.log2(math.e)

SUBLANES = 8
BF16_SUBLANES = 16
ACC_ROWS = HEAD_LANES + BF16_SUBLANES
VMEM_LIMIT_BYTES = 56 * 1024 * 1024

FFN_TM = 1024
FFN_TF = 512
INPROJ_TM = 1024
INPROJ_TN = 1024
MERGE_TM = 512
ATTN_TK = 512
ATTN_COLS = 512
ATTN_HEADS_PER_STEP = 4
DIFF_TQ = ATTN_COLS // 2
MOBA_TQ = ATTN_COLS

ONCE = dict(pipeline_mode=pl.Buffered(1))


def _params(*sem):
    return pltpu.CompilerParams(dimension_semantics=sem, vmem_limit_bytes=VMEM_LIMIT_BYTES)


def _rms(x, g, eps):
    return x * lax.rsqrt(jnp.mean(x * x, axis=-1, keepdims=True) + eps) * g


def _ffn_body(x_ref, pre_g_ref, wg_ref, wu_ref, wd_ref, post_g_ref, o_ref, h_ref, a_ref, *, nf):
    j = pl.program_id(1)

    def up(slot):
        h = h_ref[...]
        g = jnp.dot(h, wg_ref[...], preferred_element_type=F32)
        u = jnp.dot(h, wu_ref[...], preferred_element_type=F32)
        a_ref[slot] = (g * jax.nn.sigmoid(g) * u).astype(BF16)

    def down(slot):
        o_ref[...] += jnp.dot(a_ref[slot], wd_ref[...], preferred_element_type=F32)

    @pl.when(j == 0)
    def _():
        h_ref[...] = _rms(x_ref[...], pre_g_ref[...], NORM_EPS).astype(BF16)
        o_ref[...] = jnp.zeros_like(o_ref)
        up(0)

    middle = (j > 0) & (j < nf)

    @pl.when(middle & (j % 2 == 1))
    def _():
        up(1)
        down(0)

    @pl.when(middle & (j % 2 == 0))
    def _():
        up(0)
        down(1)

    @pl.when(j == nf)
    def _():
        down((nf - 1) % 2)
        o_ref[...] = x_ref[...] + 0.5 * _rms(o_ref[...], post_g_ref[...], NORM_EPS)


def _ffn(x, pre_g, w_gu, w_down, post_g, layer):
    m, d = x.shape
    f = w_down.shape[1]
    tm, tf = min(FFN_TM, m), min(FFN_TF, f)
    nf = f // tf
    up_blk = lambda j: jnp.minimum(j, nf - 1)
    down_blk = lambda j: jnp.maximum(j - 1, 0)
    return pl.pallas_call(
        functools.partial(_ffn_body, nf=nf),
        grid=(m // tm, nf + 1),
        in_specs=[
            pl.BlockSpec((tm, d), lambda i, j: (i, 0)),
            pl.BlockSpec((None, 1, d), lambda i, j: (layer, 0, 0)),
            pl.BlockSpec((None, d, tf), lambda i, j: (layer, 0, up_blk(j))),
            pl.BlockSpec((None, d, tf), lambda i, j: (layer, 0, up_blk(j) + nf)),
            pl.BlockSpec((None, tf, d), lambda i, j: (layer, down_blk(j), 0)),
            pl.BlockSpec((None, 1, d), lambda i, j: (layer, 0, 0)),
        ],
        out_specs=pl.BlockSpec((tm, d), lambda i, j: (i, 0)),
        out_shape=jax.ShapeDtypeStruct((m, d), F32),
        scratch_shapes=[
            pltpu.VMEM((tm, d), BF16),
            pltpu.VMEM((2, tm, tf), BF16),
        ],
        compiler_params=_params("arbitrary", "arbitrary"),
        name="ffn",
    )(x, pre_g, w_gu, w_gu, w_down, post_g)


def _inproj_body(x_ref, g_ref, w_ref, o_ref, h_ref):
    @pl.when(pl.program_id(1) == 0)
    def _():
        h_ref[...] = _rms(x_ref[...], g_ref[...], NORM_EPS).astype(BF16)

    o_ref[...] = jnp.dot(h_ref[...], w_ref[...], preferred_element_type=F32).astype(BF16)


def _inproj(x, g, w_in, layer):
    m, d = x.shape
    width = w_in.shape[2]
    tm = min(INPROJ_TM, m)
    tn = min(INPROJ_TN, ATTN_WIDTH)
    return pl.pallas_call(
        _inproj_body,
        grid=(m // tm, width // tn),
        in_specs=[
            pl.BlockSpec((tm, d), lambda i, j: (i, 0)),
            pl.BlockSpec((None, 1, d), lambda i, j: (layer, 0, 0)),
            pl.BlockSpec((None, d, tn), lambda i, j: (layer, 0, j)),
        ],
        out_specs=pl.BlockSpec((tm, tn), lambda i, j: (i, j)),
        out_shape=jax.ShapeDtypeStruct((m, width), BF16),
        scratch_shapes=[pltpu.VMEM((tm, d), BF16)],
        compiler_params=_params("arbitrary", "arbitrary"),
        name="inproj",
    )(x, g, w_in)


def _rope_tables_t(seq, head_dim):
    rot = head_dim // ROT_FRAC_DIV
    inv = jnp.power(ROPE_THETA, -jnp.arange(0, rot, 2, dtype=F32) / rot)
    ang = inv[:, None] * jnp.arange(seq, dtype=F32)[None, :]
    return jnp.cos(ang), jnp.sin(ang)


def _rope_t(xt, cos, sin, head_dim):
    half = cos.shape[0]
    assert half % SUBLANES == 0
    parts = []
    for base in range(0, xt.shape[0], head_dim):
        x1 = xt[base:base + half]
        x2 = xt[base + half:base + 2 * half]
        parts += [x1 * cos - x2 * sin, x2 * cos + x1 * sin, xt[base + 2 * half:base + head_dim]]
    return jnp.concatenate(parts, axis=0)


class _Stream(NamedTuple):
    scores: Callable[[Any], Any]
    diag_bias: Callable[[], Any]
    vt: Any
    s: Any
    mb: Any
    m: Any
    acc: Any


def _head_scratch(heads, seq, tk, cols, key_lanes):
    return [
        pltpu.VMEM((heads, key_lanes, cols), BF16),
        pltpu.VMEM((heads, seq, key_lanes), BF16),
        pltpu.VMEM((heads, seq // tk, ACC_ROWS, tk), BF16),
        pltpu.VMEM((heads, 2, tk, cols), F32),
        pltpu.VMEM((heads, 2, 1, cols), F32),
        pltpu.VMEM((heads, 1, cols), F32),
        pltpu.VMEM((heads, ACC_ROWS, cols), F32),
    ]


def _prepare_keys_values(k_ref, v_ref, cos_ref, sin_ref, kr_ref, vt_ref, lanes, seq, tk, head_dim):
    extra = lax.broadcasted_iota(jnp.int32, (ACC_ROWS - HEAD_LANES, tk), 0)
    ones_row = jnp.where(extra == 0, 1.0, 0.0).astype(BF16)
    for c in range(seq // tk):
        rows = slice(c * tk, (c + 1) * tk)
        kt = _rope_t(k_ref[rows, lanes].astype(F32).T, cos_ref[:, rows], sin_ref[:, rows], head_dim)
        kr_ref[rows, :HEAD_LANES] = kt.T.astype(BF16)
        vt_ref[c, :HEAD_LANES, :] = v_ref[rows, lanes].astype(F32).T.astype(BF16)
        vt_ref[c, HEAD_LANES:, :] = ones_row


def _chunk_rows(kb, tk):
    start = kb * tk
    return pl.ds(start if isinstance(kb, int) else pl.multiple_of(start, tk), tk)


def _attend(n_past, streams, prologue, epilogue):
    def produce(kb, slot, past):
        for st in streams:
            sc = st.scores(kb)
            if past:
                st.mb[slot] = jnp.max(sc, axis=0, keepdims=True)
            st.s[slot] = sc

    def consume(kb, slot, diag):
        for st in streams:
            sc = st.s[slot]
            vt_blk = st.vt[kb]
            if diag:
                sc = sc + st.diag_bias()
                m_new = jnp.max(sc, axis=0, keepdims=True)
                p = jnp.exp2(sc - m_new).astype(BF16)
                st.acc[...] = jnp.dot(vt_blk, p, preferred_element_type=F32)
            else:
                m_old = st.m[...]
                m_new = jnp.maximum(m_old, st.mb[slot])
                alpha = jnp.exp2(m_old - m_new)
                p = jnp.exp2(sc - m_new).astype(BF16)
                st.acc[...] = alpha * st.acc[...] + jnp.dot(vt_blk, p, preferred_element_type=F32)
            st.m[...] = m_new

    @pl.when(n_past == 0)
    def _():
        prologue()
        produce(0, 0, False)
        consume(0, 0, True)
        epilogue()

    @pl.when(n_past > 0)
    def _():
        prologue()
        produce(n_past, 0, False)
        produce(0, 1, True)
        consume(n_past, 0, True)
        rest = n_past - 1
        pairs = rest // 2

        def pair(t, carry):
            c = 2 * t + 1
            produce(c, 0, True)
            consume(c - 1, 1, False)
            produce(c + 1, 1, True)
            consume(c, 0, False)
            return carry

        lax.fori_loop(0, pairs, pair, 0)
        last = 2 * pairs

        @pl.when(rest % 2 == 1)
        def _():
            produce(last + 1, 0, True)
            consume(last, 1, False)
            consume(last + 1, 0, False)
            epilogue()

        @pl.when(rest % 2 == 0)
        def _():
            consume(last, 1, False)
            epilogue()


def _diff_causal_bias(tq, tk):
    p = lax.broadcasted_iota(jnp.int32, (tk // tq, tk, 2 * tq), 0)
    r = lax.broadcasted_iota(jnp.int32, (tk // tq, tk, 2 * tq), 1)
    c = lax.broadcasted_iota(jnp.int32, (tk // tq, tk, 2 * tq), 2) % tq
    return jnp.where(r <= p * tq + c, 0.0, NEG_INF).astype(F32)


def _diff_body(lq1_ref, lk1_ref, lq2_ref, lk2_ref, sg_ref, cq_ref, sq_ref, ck_ref, sk_ref, bias_ref,
               q_ref, k_ref, v_ref, o_ref,
               qt_ref, kr_ref, vt_ref, s_ref, mb_ref, m_ref, acc_ref,
               *, heads, seq, tq, tk, lambda_init):
    qi = pl.program_id(2)
    d = DIFF_HEAD_DIM
    head_lanes = [slice(hh * HEAD_LANES, (hh + 1) * HEAD_LANES) for hh in range(heads)]

    @pl.when(qi == 0)
    def _():
        for hh in range(heads):
            _prepare_keys_values(k_ref, v_ref, ck_ref, sk_ref, kr_ref.at[hh], vt_ref.at[hh],
                                 head_lanes[hh], seq, tk, d)

    def prologue():
        row = lax.broadcasted_iota(jnp.int32, (2 * d, tq), 0)
        for hh in range(heads):
            qt = _rope_t(q_ref[:, head_lanes[hh]].astype(F32).T, cq_ref[...], sq_ref[...], d)
            qt = qt * (d ** -0.5 * LOG2E)
            qt_ref[hh, :, :tq] = jnp.where(row < d, qt, 0.0).astype(BF16)
            qt_ref[hh, :, tq:] = jnp.where(row >= d, qt, 0.0).astype(BF16)

    def make_stream(hh):
        def scores(kb):
            k_blk = kr_ref[hh, _chunk_rows(kb, tk), :]
            return jnp.dot(k_blk, qt_ref[hh], preferred_element_type=F32)

        return _Stream(scores, lambda: bias_ref[qi % (tk // tq)], vt_ref.at[hh], s_ref.at[hh],
                       mb_ref.at[hh], m_ref.at[hh], acc_ref.at[hh])

    def epilogue():
        lam = (jnp.exp(jnp.sum(lq1_ref[...] * lk1_ref[...], axis=-1, keepdims=True))
               - jnp.exp(jnp.sum(lq2_ref[...] * lk2_ref[...], axis=-1, keepdims=True))
               + lambda_init)
        for hh in range(heads):
            denom = acc_ref[hh, HEAD_LANES:HEAD_LANES + 1, :]
            acc = acc_ref[hh, :HEAD_LANES, :] * (1.0 / denom)
            o = (acc[:, :tq] - lam * acc[:, tq:]).T
            o_ref[:, head_lanes[hh]] = (
                _rms(o, sg_ref[...], SUBLN_EPS) * (1.0 - lambda_init)).astype(BF16)

    _attend((qi * tq) // tk, [make_stream(hh) for hh in range(heads)], prologue, epilogue)


def _diff_attention(proj3, cos_t, sin_t, lq1, lk1, lq2, lk2, subln_g, layer, lambda_init):
    b, seq, _ = proj3.shape
    tq, tk = min(DIFF_TQ, seq), min(ATTN_TK, seq)
    heads = ATTN_HEADS_PER_STEP
    bw = heads * HEAD_LANES
    half = cos_t.shape[0]
    kcol = ATTN_WIDTH // bw
    vcol = 2 * ATTN_WIDTH // bw
    lspec = pl.BlockSpec((None, 1, DIFF_HEAD_DIM), lambda bi, h, qi: (layer, 0, 0))
    tab_q = pl.BlockSpec((half, tq), lambda bi, h, qi: (0, qi))
    tab_k = pl.BlockSpec((half, seq), lambda bi, h, qi: (0, 0))
    return pl.pallas_call(
        functools.partial(_diff_body, heads=heads, seq=seq, tq=tq, tk=tk, lambda_init=lambda_init),
        grid=(b, DIFF_HEADS // heads, seq // tq),
        in_specs=[
            lspec, lspec, lspec, lspec,
            pl.BlockSpec((None, 1, HEAD_LANES), lambda bi, h, qi: (layer, 0, 0)),
            tab_q, tab_q, tab_k, tab_k,
            pl.BlockSpec((tk // tq, tk, 2 * tq), lambda bi, h, qi: (0, 0, 0), **ONCE),
            pl.BlockSpec((None, tq, bw), lambda bi, h, qi: (bi, qi, h)),
            pl.BlockSpec((None, seq, bw), lambda bi, h, qi: (bi, 0, kcol + h)),
            pl.BlockSpec((None, seq, bw), lambda bi, h, qi: (bi, 0, vcol + h)),
        ],
        out_specs=pl.BlockSpec((None, tq, bw), lambda bi, h, qi: (bi, qi, h)),
        out_shape=jax.ShapeDtypeStruct((b, seq, ATTN_WIDTH), BF16),
        scratch_shapes=_head_scratch(heads, seq, tk, 2 * tq, HEAD_LANES),
        compiler_params=_params("arbitrary", "arbitrary", "arbitrary"),
        name="diff_attn",
    )(lq1, lk1, lq2, lk2, subln_g, cos_t, sin_t, cos_t, sin_t, _diff_causal_bias(tq, tk),
      proj3, proj3, proj3)


def _block_of(pos):
    return jnp.right_shift(pos, MOBA_BLOCK.bit_length() - 1)


def _moba_diag_bias(tq):
    r = lax.broadcasted_iota(jnp.int32, (tq, tq), 0)
    c = lax.broadcasted_iota(jnp.int32, (tq, tq), 1)
    r_blk, c_blk = _block_of(r), _block_of(c)
    keep = ((r <= c) & (r_blk == c_blk)) | (r_blk < c_blk)
    return jnp.where(keep, 0.0, NEG_INF).astype(F32)


def _moba_body(cq_ref, sq_ref, ck_ref, sk_ref, bias_ref, q_ref, k_ref, v_ref, o_ref,
               qt_ref, kr_ref, vt_ref, s_ref, mb_ref, m_ref, acc_ref, km_ref,
               *, heads, seq, tq, tk):
    qi = pl.program_id(2)
    blk = MOBA_BLOCK
    hd = MOBA_HEAD_DIM
    nb = seq // blk
    assert nb <= HEAD_LANES
    head_lanes = [slice(hh * HEAD_LANES, (hh + 1) * HEAD_LANES) for hh in range(heads)]

    @pl.when(qi == 0)
    def _():
        n_i = lax.broadcasted_iota(jnp.int32, (nb, seq), 0)
        s_i = lax.broadcasted_iota(jnp.int32, (nb, seq), 1)
        ind = jnp.where(_block_of(s_i) == n_i, 1.0 / blk, 0.0).astype(BF16)
        r_i = lax.broadcasted_iota(jnp.int32, (tk, HEAD_LANES), 0)
        l_i = lax.broadcasted_iota(jnp.int32, (tk, HEAD_LANES), 1)
        for hh in range(heads):
            _prepare_keys_values(k_ref, v_ref, ck_ref, sk_ref, kr_ref.at[hh], vt_ref.at[hh],
                                 head_lanes[hh], seq, tk, hd)
            for c in range(seq // tk):
                onehot = jnp.where(_block_of(c * tk + r_i) == l_i, 1.0, 0.0).astype(BF16)
                kr_ref[hh, c * tk:(c + 1) * tk, HEAD_LANES:] = onehot
            kmean = jnp.dot(ind, kr_ref[hh, :, :HEAD_LANES], preferred_element_type=F32)
            hi = kmean.astype(BF16)
            km_ref[hh, :nb, :] = hi
            km_ref[hh, nb:, :] = (kmean - hi.astype(F32)).astype(BF16)

    def prologue():
        n_i = lax.broadcasted_iota(jnp.int32, (nb, tq), 0)
        c_i = lax.broadcasted_iota(jnp.int32, (nb, tq), 1)
        is_past = n_i < _block_of(qi * tq + c_i)
        for hh in range(heads):
            qt = _rope_t(q_ref[:, head_lanes[hh]].astype(F32).T, cq_ref[...], sq_ref[...], hd)
            qt = (qt * (hd ** -0.5 * LOG2E)).astype(BF16)
            qt_ref[hh, :HEAD_LANES, :] = qt

            g2 = jnp.dot(km_ref[hh], qt, preferred_element_type=F32)
            gm = jnp.where(is_past, g2[:nb, :] + g2[nb:, :], NEG_INF)
            gate = jnp.where(is_past, NEG_INF, 0.0)
            for _ in range(min(MOBA_TOPK, nb)):
                mx = jnp.max(gm, axis=0, keepdims=True)
                first = jnp.min(jnp.where(gm == mx, n_i, nb), axis=0, keepdims=True)
                sel = n_i == first
                gate = jnp.where(sel, 0.0, gate)
                gm = jnp.where(sel, REMOVED, gm)
            qt_ref[hh, HEAD_LANES:, :] = jnp.concatenate(
                [gate, jnp.zeros((HEAD_LANES - nb, tq), F32)], axis=0).astype(BF16)

    def make_stream(hh):
        def scores(kb):
            k_blk = kr_ref[hh, _chunk_rows(kb, tk), :]
            return jnp.dot(k_blk, qt_ref[hh], preferred_element_type=F32)

        return _Stream(scores, lambda: bias_ref[...], vt_ref.at[hh], s_ref.at[hh],
                       mb_ref.at[hh], m_ref.at[hh], acc_ref.at[hh])

    def epilogue():
        for hh in range(heads):
            denom = acc_ref[hh, HEAD_LANES:HEAD_LANES + 1, :]
            o = acc_ref[hh, :HEAD_LANES, :] * (1.0 / denom)
            o_ref[:, head_lanes[hh]] = o.T.astype(BF16)

    _attend(qi, [make_stream(hh) for hh in range(heads)], prologue, epilogue)


def _moba_attention(proj3, cos_t, sin_t):
    b, seq, _ = proj3.shape
    tq = tk = min(MOBA_TQ, seq)
    heads = ATTN_HEADS_PER_STEP
    bw = heads * HEAD_LANES
    nb = seq // MOBA_BLOCK
    half = cos_t.shape[0]
    qcol = 3 * ATTN_WIDTH // bw
    kcol = 4 * ATTN_WIDTH // bw
    vcol = 5 * ATTN_WIDTH // bw
    tab_q = pl.BlockSpec((half, tq), lambda bi, h, qi: (0, qi))
    tab_k = pl.BlockSpec((half, seq), lambda bi, h, qi: (0, 0))
    return pl.pallas_call(
        functools.partial(_moba_body, heads=heads, seq=seq, tq=tq, tk=tk),
        grid=(b, MOBA_HEADS // heads, seq // tq),
        in_specs=[
            tab_q, tab_q, tab_k, tab_k,
            pl.BlockSpec((tq, tq), lambda bi, h, qi: (0, 0), **ONCE),
            pl.BlockSpec((None, tq, bw), lambda bi, h, qi: (bi, qi, qcol + h)),
            pl.BlockSpec((None, seq, bw), lambda bi, h, qi: (bi, 0, kcol + h)),
            pl.BlockSpec((None, seq, bw), lambda bi, h, qi: (bi, 0, vcol + h)),
        ],
        out_specs=pl.BlockSpec((None, tq, bw), lambda bi, h, qi: (bi, qi, h)),
        out_shape=jax.ShapeDtypeStruct((b, seq, ATTN_WIDTH), BF16),
        scratch_shapes=_head_scratch(heads, seq, tk, tq, 2 * HEAD_LANES) + [
            pltpu.VMEM((heads, 2 * nb, MOBA_HEAD_DIM), BF16),
        ],
        compiler_params=_params("arbitrary", "arbitrary", "arbitrary"),
        name="moba_attn",
    )(cos_t, sin_t, cos_t, sin_t, _moba_diag_bias(tq), proj3, proj3, proj3)


def _merge_body(ya_ref, yb_ref, ga_ref, gb_ref, x_ref, wa_ref, wb_ref, wo_ref, g_ref, o_ref):
    ta = jnp.dot(ya_ref[...], wa_ref[...], preferred_element_type=F32)
    tb = jnp.dot(yb_ref[...], wb_ref[...], preferred_element_type=F32)
    gate_a = jax.nn.sigmoid(ga_ref[...].astype(F32))
    gate_b = jax.nn.sigmoid(gb_ref[...].astype(F32))
    merged = (gate_a * ta + gate_b * tb).astype(BF16)
    y = jnp.dot(merged, wo_ref[...], preferred_element_type=F32)
    o_ref[...] = x_ref[...] + _rms(y, g_ref[...], NORM_EPS)


def _merge(ya, yb, proj, x, w_a, w_b, w_o, g, layer):
    m, d = x.shape
    tm = min(MERGE_TM, m)
    gate_col = 6 * ATTN_WIDTH // d
    return pl.pallas_call(
        _merge_body,
        grid=(m // tm,),
        in_specs=[
            pl.BlockSpec((tm, ATTN_WIDTH), lambda i: (i, 0)),
            pl.BlockSpec((tm, ATTN_WIDTH), lambda i: (i, 0)),
            pl.BlockSpec((tm, d), lambda i: (i, gate_col)),
            pl.BlockSpec((tm, d), lambda i: (i, gate_col + 1)),
            pl.BlockSpec((tm, d), lambda i: (i, 0)),
            pl.BlockSpec((None, ATTN_WIDTH, d), lambda i: (layer, 0, 0), **ONCE),
            pl.BlockSpec((None, ATTN_WIDTH, d), lambda i: (layer, 0, 0), **ONCE),
            pl.BlockSpec((None, d, d), lambda i: (layer, 0, 0), **ONCE),
            pl.BlockSpec((None, 1, d), lambda i: (layer, 0, 0)),
        ],
        out_specs=pl.BlockSpec((tm, d), lambda i: (i, 0)),
        out_shape=jax.ShapeDtypeStruct((m, d), F32),
        compiler_params=_params("arbitrary"),
        name="merge",
    )(ya, yb, proj, proj, x, w_a, w_b, w_o, g)


def _trunk(x, ffn1_pre_g, ffn1_w_gu, ffn1_w_down, ffn1_post_g, mix_pre_g, w_in,
           diff_lq1, diff_lk1, diff_lq2, diff_lk2, diff_subln_g,
           w_branch_diff, w_branch_moba, w_out, mix_post_g,
           ffn2_pre_g, ffn2_w_gu, ffn2_w_down, ffn2_post_g):
    b, seq, d = x.shape
    depth = w_in.shape[0]
    assert (6 * ATTN_WIDTH) % d == 0 and seq % ATTN_TK == 0
    row = lambda p: p[:, None, :]
    bf = lambda w: w.astype(BF16)
    cos_d, sin_d = _rope_tables_t(seq, DIFF_HEAD_DIM)
    cos_m, sin_m = _rope_tables_t(seq, MOBA_HEAD_DIM)
    w1gu, w1d, w2gu, w2d = bf(ffn1_w_gu), bf(ffn1_w_down), bf(ffn2_w_gu), bf(ffn2_w_down)
    win, wa, wb, wo = bf(w_in), bf(w_branch_diff), bf(w_branch_moba), bf(w_out)
    xt = x.reshape(b * seq, d)
    for l in range(depth):
        lambda_init = 0.8 - 0.6 * math.exp(-0.3 * l)
        xt = _ffn(xt, row(ffn1_pre_g), w1gu, w1d, row(ffn1_post_g), l)
        proj = _inproj(xt, row(mix_pre_g), win, l)
        proj3 = proj.reshape(b, seq, proj.shape[1])
        ya = _diff_attention(proj3, cos_d, sin_d, row(diff_lq1), row(diff_lk1), row(diff_lq2),
                             row(diff_lk2), row(diff_subln_g), l, lambda_init)
        yb = _moba_attention(proj3, cos_m, sin_m)
        xt = _merge(ya.reshape(b * seq, ATTN_WIDTH), yb.reshape(b * seq, ATTN_WIDTH), proj, xt,
                    wa, wb, wo, row(mix_post_g), l)
        xt = _ffn(xt, row(ffn2_pre_g), w2gu, w2d, row(ffn2_post_g), l)
    return xt.reshape(b, seq, d)


def kernel(x, ffn1_pre_g, ffn1_w_gu, ffn1_w_down, ffn1_post_g, mix_pre_g, w_in, diff_lq1, diff_lk1, diff_lq2, diff_lk2, diff_subln_g, w_branch_diff, w_branch_moba, w_out, mix_post_g, ffn2_pre_g, ffn2_w_gu, ffn2_w_down, ffn2_post_g):
    return _trunk(x, ffn1_pre_g, ffn1_w_gu, ffn1_w_down, ffn1_post_g, mix_pre_g, w_in,
                  diff_lq1, diff_lk1, diff_lq2, diff_lk2, diff_subln_g,
                  w_branch_diff, w_branch_moba, w_out, mix_post_g,
                  ffn2_pre_g, ffn2_w_gu, ffn2_w_down, ffn2_post_g)
```

```python
import functools
import math
from typing import Any, Callable, NamedTuple

import jax
import jax.numpy as jnp
from jax import lax
from jax.experimental import pallas as pl
from jax.experimental.pallas import tpu as pltpu

F32 = jnp.float32
BF16 = jnp.bfloat16

DIFF_HEADS = 8
DIFF_HEAD_DIM = 64
MOBA_HEADS = 8
MOBA_HEAD_DIM = 128
MOBA_BLOCK = 256
MOBA_TOPK = 3
ATTN_WIDTH = 1024
HEAD_LANES = 128
ROPE_THETA = 500000.0
ROT_FRAC_DIV = 4
NORM_EPS = 1e-6
SUBLN_EPS = 1e-5
NEG_INF = -1e30
REMOVED = -3e38
LOG2E = math.log2(math.e)

SUBLANES = 8
BF16_SUBLANES = 16
ACC_ROWS = HEAD_LANES + BF16_SUBLANES
VMEM_LIMIT_BYTES = 56 * 1024 * 1024

FFN_TM = 1024
FFN_TF = 512
INPROJ_TM = 1024
INPROJ_TN = 2048
MERGE_TM = 512
ATTN_TK = 512
ATTN_COLS = 512
ATTN_HEADS_PER_STEP = 4
DIFF_TQ = ATTN_COLS // 2
MOBA_TQ = ATTN_COLS

ONCE = dict(pipeline_mode=pl.Buffered(1))


def _params(*sem):
    return pltpu.CompilerParams(dimension_semantics=sem, vmem_limit_bytes=VMEM_LIMIT_BYTES)


def _rms(x, g, eps):
    return x * lax.rsqrt(jnp.mean(x * x, axis=-1, keepdims=True) + eps) * g


def _ffn_body(x_ref, pre_g_ref, wg_ref, wu_ref, wd_ref, post_g_ref, o_ref, h_ref, a_ref, *, nf):
    j = pl.program_id(1)

    def up(slot):
        h = h_ref[...]
        g = jnp.dot(h, wg_ref[...], preferred_element_type=F32)
        u = jnp.dot(h, wu_ref[...], preferred_element_type=F32)
        a_ref[slot] = (g * jax.nn.sigmoid(g) * u).astype(BF16)

    def down(slot):
        o_ref[...] += jnp.dot(a_ref[slot], wd_ref[...], preferred_element_type=F32)

    @pl.when(j == 0)
    def _():
        h_ref[...] = _rms(x_ref[...], pre_g_ref[...], NORM_EPS).astype(BF16)
        o_ref[...] = jnp.zeros_like(o_ref)
        up(0)

    middle = (j > 0) & (j < nf)

    @pl.when(middle & (j % 2 == 1))
    def _():
        up(1)
        down(0)

    @pl.when(middle & (j % 2 == 0))
    def _():
        up(0)
        down(1)

    @pl.when(j == nf)
    def _():
        down((nf - 1) % 2)
        o_ref[...] = x_ref[...] + 0.5 * _rms(o_ref[...], post_g_ref[...], NORM_EPS)


def _ffn(x, pre_g, w_gu, w_down, post_g, layer):
    m, d = x.shape
    f = w_down.shape[1]
    tm, tf = min(FFN_TM, m), min(FFN_TF, f)
    nf = f // tf
    up_blk = lambda j: jnp.minimum(j, nf - 1)
    down_blk = lambda j: jnp.maximum(j - 1, 0)
    return pl.pallas_call(
        functools.partial(_ffn_body, nf=nf),
        grid=(m // tm, nf + 1),
        in_specs=[
            pl.BlockSpec((tm, d), lambda i, j: (i, 0)),
            pl.BlockSpec((None, 1, d), lambda i, j: (layer, 0, 0)),
            pl.BlockSpec((None, d, tf), lambda i, j: (layer, 0, up_blk(j))),
            pl.BlockSpec((None, d, tf), lambda i, j: (layer, 0, up_blk(j) + nf)),
            pl.BlockSpec((None, tf, d), lambda i, j: (layer, down_blk(j), 0)),
            pl.BlockSpec((None, 1, d), lambda i, j: (layer, 0, 0)),
        ],
        out_specs=pl.BlockSpec((tm, d), lambda i, j: (i, 0)),
        out_shape=jax.ShapeDtypeStruct((m, d), F32),
        scratch_shapes=[
            pltpu.VMEM((tm, d), BF16),
            pltpu.VMEM((2, tm, tf), BF16),
        ],
        compiler_params=_params("arbitrary", "arbitrary"),
        name="ffn",
    )(x, pre_g, w_gu, w_gu, w_down, post_g)


def _inproj_body(x_ref, g_ref, w_ref, o_ref, h_ref):
    @pl.when(pl.program_id(1) == 0)
    def _():
        h_ref[...] = _rms(x_ref[...], g_ref[...], NORM_EPS).astype(BF16)

    o_ref[...] = jnp.dot(h_ref[...], w_ref[...], preferred_element_type=F32).astype(BF16)


def _inproj(x, g, w_in, layer):
    m, d = x.shape
    width = w_in.shape[2]
    tm = min(INPROJ_TM, m)
    tn = INPROJ_TN if width % INPROJ_TN == 0 else ATTN_WIDTH
    return pl.pallas_call(
        _inproj_body,
        grid=(m // tm, width // tn),
        in_specs=[
            pl.BlockSpec((tm, d), lambda i, j: (i, 0)),
            pl.BlockSpec((None, 1, d), lambda i, j: (layer, 0, 0)),
            pl.BlockSpec((None, d, tn), lambda i, j: (layer, 0, j)),
        ],
        out_specs=pl.BlockSpec((tm, tn), lambda i, j: (i, j)),
        out_shape=jax.ShapeDtypeStruct((m, width), BF16),
        scratch_shapes=[pltpu.VMEM((tm, d), BF16)],
        compiler_params=_params("arbitrary", "arbitrary"),
        name="inproj",
    )(x, g, w_in)


def _rope_tables_t(seq, head_dim):
    rot = head_dim // ROT_FRAC_DIV
    inv = jnp.power(ROPE_THETA, -jnp.arange(0, rot, 2, dtype=F32) / rot)
    ang = inv[:, None] * jnp.arange(seq, dtype=F32)[None, :]
    return jnp.cos(ang), jnp.sin(ang)


def _rope_t(xt, cos, sin, head_dim):
    half = cos.shape[0]
    assert half % SUBLANES == 0
    parts = []
    for base in range(0, xt.shape[0], head_dim):
        x1 = xt[base:base + half]
        x2 = xt[base + half:base + 2 * half]
        parts += [x1 * cos - x2 * sin, x2 * cos + x1 * sin, xt[base + 2 * half:base + head_dim]]
    return jnp.concatenate(parts, axis=0)


class _Stream(NamedTuple):
    scores: Callable[[Any], Any]
    diag_bias: Callable[[], Any]
    vt: Any
    s: Any
    mb: Any
    m: Any
    acc: Any


def _head_scratch(heads, seq, tk, cols, key_lanes):
    return [
        pltpu.VMEM((heads, key_lanes, cols), BF16),
        pltpu.VMEM((heads, seq, key_lanes), BF16),
        pltpu.VMEM((heads, seq // tk, ACC_ROWS, tk), BF16),
        pltpu.VMEM((heads, 2, tk, cols), F32),
        pltpu.VMEM((heads, 2, 1, cols), F32),
        pltpu.VMEM((heads, 1, cols), F32),
        pltpu.VMEM((heads, ACC_ROWS, cols), F32),
    ]


def _prepare_keys_values(k_ref, v_ref, cos_ref, sin_ref, kr_ref, vt_ref, lanes, seq, tk, head_dim):
    extra = lax.broadcasted_iota(jnp.int32, (ACC_ROWS - HEAD_LANES, tk), 0)
    ones_row = jnp.where(extra == 0, 1.0, 0.0).astype(BF16)
    for c in range(seq // tk):
        rows = slice(c * tk, (c + 1) * tk)
        kt = _rope_t(k_ref[rows, lanes].astype(F32).T, cos_ref[:, rows], sin_ref[:, rows], head_dim)
        kr_ref[rows, :HEAD_LANES] = kt.T.astype(BF16)
        vt_ref[c, :HEAD_LANES, :] = v_ref[rows, lanes].astype(F32).T.astype(BF16)
        vt_ref[c, HEAD_LANES:, :] = ones_row


def _chunk_rows(kb, tk):
    start = kb * tk
    return pl.ds(start if isinstance(kb, int) else pl.multiple_of(start, tk), tk)


def _attend(n_past, streams, prologue, epilogue):
    def produce(kb, slot, past):
        for st in streams:
            sc = st.scores(kb)
            if past:
                st.mb[slot] = jnp.max(sc, axis=0, keepdims=True)
            st.s[slot] = sc

    def consume(kb, slot, diag):
        for st in streams:
            sc = st.s[slot]
            vt_blk = st.vt[kb]
            if diag:
                sc = sc + st.diag_bias()
                m_new = jnp.max(sc, axis=0, keepdims=True)
                p = jnp.exp2(sc - m_new).astype(BF16)
                st.acc[...] = jnp.dot(vt_blk, p, preferred_element_type=F32)
            else:
                m_old = st.m[...]
                m_new = jnp.maximum(m_old, st.mb[slot])
                alpha = jnp.exp2(m_old - m_new)
                p = jnp.exp2(sc - m_new).astype(BF16)
                st.acc[...] = alpha * st.acc[...] + jnp.dot(vt_blk, p, preferred_element_type=F32)
            st.m[...] = m_new

    @pl.when(n_past == 0)
    def _():
        prologue()
        produce(0, 0, False)
        consume(0, 0, True)
        epilogue()

    @pl.when(n_past > 0)
    def _():
        prologue()
        produce(n_past, 0, False)
        produce(0, 1, True)
        consume(n_past, 0, True)
        rest = n_past - 1
        pairs = rest // 2

        def pair(t, carry):
            c = 2 * t + 1
            produce(c, 0, True)
            consume(c - 1, 1, False)
            produce(c + 1, 1, True)
            consume(c, 0, False)
            return carry

        lax.fori_loop(0, pairs, pair, 0)
        last = 2 * pairs

        @pl.when(rest % 2 == 1)
        def _():
            produce(last + 1, 0, True)
            consume(last, 1, False)
            consume(last + 1, 0, False)
            epilogue()

        @pl.when(rest % 2 == 0)
        def _():
            consume(last, 1, False)
            epilogue()


def _diff_causal_bias(tq, tk):
    p = lax.broadcasted_iota(jnp.int32, (tk // tq, tk, 2 * tq), 0)
    r = lax.broadcasted_iota(jnp.int32, (tk // tq, tk, 2 * tq), 1)
    c = lax.broadcasted_iota(jnp.int32, (tk // tq, tk, 2 * tq), 2) % tq
    return jnp.where(r <= p * tq + c, 0.0, NEG_INF).astype(F32)


def _diff_body(lq1_ref, lk1_ref, lq2_ref, lk2_ref, sg_ref, cq_ref, sq_ref, ck_ref, sk_ref, bias_ref,
               q_ref, k_ref, v_ref, o_ref,
               qt_ref, kr_ref, vt_ref, s_ref, mb_ref, m_ref, acc_ref,
               *, heads, seq, tq, tk, lambda_init):
    qi = pl.program_id(2)
    d = DIFF_HEAD_DIM
    head_lanes = [slice(hh * HEAD_LANES, (hh + 1) * HEAD_LANES) for hh in range(heads)]

    @pl.when(qi == 0)
    def _():
        for hh in range(heads):
            _prepare_keys_values(k_ref, v_ref, ck_ref, sk_ref, kr_ref.at[hh], vt_ref.at[hh],
                                 head_lanes[hh], seq, tk, d)

    def prologue():
        row = lax.broadcasted_iota(jnp.int32, (2 * d, tq), 0)
        for hh in range(heads):
            qt = _rope_t(q_ref[:, head_lanes[hh]].astype(F32).T, cq_ref[...], sq_ref[...], d)
            qt = qt * (d ** -0.5 * LOG2E)
            qt_ref[hh, :, :tq] = jnp.where(row < d, qt, 0.0).astype(BF16)
            qt_ref[hh, :, tq:] = jnp.where(row >= d, qt, 0.0).astype(BF16)

    def make_stream(hh):
        def scores(kb):
            k_blk = kr_ref[hh, _chunk_rows(kb, tk), :]
            return jnp.dot(k_blk, qt_ref[hh], preferred_element_type=F32)

        return _Stream(scores, lambda: bias_ref[qi % (tk // tq)], vt_ref.at[hh], s_ref.at[hh],
                       mb_ref.at[hh], m_ref.at[hh], acc_ref.at[hh])

    def epilogue():
        lam = (jnp.exp(jnp.sum(lq1_ref[...] * lk1_ref[...], axis=-1, keepdims=True))
               - jnp.exp(jnp.sum(lq2_ref[...] * lk2_ref[...], axis=-1, keepdims=True))
               + lambda_init)
        for hh in range(heads):
            denom = acc_ref[hh, HEAD_LANES:HEAD_LANES + 1, :]
            acc = acc_ref[hh, :HEAD_LANES, :] * (1.0 / denom)
            o = (acc[:, :tq] - lam * acc[:, tq:]).T
            o_ref[:, head_lanes[hh]] = (
                _rms(o, sg_ref[...], SUBLN_EPS) * (1.0 - lambda_init)).astype(BF16)

    _attend((qi * tq) // tk, [make_stream(hh) for hh in range(heads)], prologue, epilogue)


def _diff_attention(proj3, cos_t, sin_t, lq1, lk1, lq2, lk2, subln_g, layer, lambda_init):
    b, seq, _ = proj3.shape
    tq, tk = min(DIFF_TQ, seq), min(ATTN_TK, seq)
    heads = ATTN_HEADS_PER_STEP
    bw = heads * HEAD_LANES
    half = cos_t.shape[0]
    kcol = ATTN_WIDTH // bw
    vcol = 2 * ATTN_WIDTH // bw
    lspec = pl.BlockSpec((None, 1, DIFF_HEAD_DIM), lambda bi, h, qi: (layer, 0, 0))
    tab_q = pl.BlockSpec((half, tq), lambda bi, h, qi: (0, qi))
    tab_k = pl.BlockSpec((half, seq), lambda bi, h, qi: (0, 0))
    return pl.pallas_call(
        functools.partial(_diff_body, heads=heads, seq=seq, tq=tq, tk=tk, lambda_init=lambda_init),
        grid=(b, DIFF_HEADS // heads, seq // tq),
        in_specs=[
            lspec, lspec, lspec, lspec,
            pl.BlockSpec((None, 1, HEAD_LANES), lambda bi, h, qi: (layer, 0, 0)),
            tab_q, tab_q, tab_k, tab_k,
            pl.BlockSpec((tk // tq, tk, 2 * tq), lambda bi, h, qi: (0, 0, 0), **ONCE),
            pl.BlockSpec((None, tq, bw), lambda bi, h, qi: (bi, qi, h)),
            pl.BlockSpec((None, seq, bw), lambda bi, h, qi: (bi, 0, kcol + h)),
            pl.BlockSpec((None, seq, bw), lambda bi, h, qi: (bi, 0, vcol + h)),
        ],
        out_specs=pl.BlockSpec((None, tq, bw), lambda bi, h, qi: (bi, qi, h)),
        out_shape=jax.ShapeDtypeStruct((b, seq, ATTN_WIDTH), BF16),
        scratch_shapes=_head_scratch(heads, seq, tk, 2 * tq, HEAD_LANES),
        compiler_params=_params("arbitrary", "arbitrary", "arbitrary"),
        name="diff_attn",
    )(lq1, lk1, lq2, lk2, subln_g, cos_t, sin_t, cos_t, sin_t, _diff_causal_bias(tq, tk),
      proj3, proj3, proj3)


def _block_of(pos):
    return jnp.right_shift(pos, MOBA_BLOCK.bit_length() - 1)


def _moba_diag_bias(tq):
    r = lax.broadcasted_iota(jnp.int32, (tq, tq), 0)
    c = lax.broadcasted_iota(jnp.int32, (tq, tq), 1)
    r_blk, c_blk = _block_of(r), _block_of(c)
    keep = ((r <= c) & (r_blk == c_blk)) | (r_blk < c_blk)
    return jnp.where(keep, 0.0, NEG_INF).astype(F32)


def _moba_body(cq_ref, sq_ref, ck_ref, sk_ref, bias_ref, q_ref, k_ref, v_ref, o_ref,
               qt_ref, kr_ref, vt_ref, s_ref, mb_ref, m_ref, acc_ref, km_ref,
               *, heads, seq, tq, tk):
    qi = pl.program_id(2)
    blk = MOBA_BLOCK
    hd = MOBA_HEAD_DIM
    nb = seq // blk
    assert nb <= HEAD_LANES
    head_lanes = [slice(hh * HEAD_LANES, (hh + 1) * HEAD_LANES) for hh in range(heads)]

    @pl.when(qi == 0)
    def _():
        n_i = lax.broadcasted_iota(jnp.int32, (nb, seq), 0)
        s_i = lax.broadcasted_iota(jnp.int32, (nb, seq), 1)
        ind = jnp.where(_block_of(s_i) == n_i, 1.0 / blk, 0.0).astype(BF16)
        r_i = lax.broadcasted_iota(jnp.int32, (tk, HEAD_LANES), 0)
        l_i = lax.broadcasted_iota(jnp.int32, (tk, HEAD_LANES), 1)
        for hh in range(heads):
            _prepare_keys_values(k_ref, v_ref, ck_ref, sk_ref, kr_ref.at[hh], vt_ref.at[hh],
                                 head_lanes[hh], seq, tk, hd)
            for c in range(seq // tk):
                onehot = jnp.where(_block_of(c * tk + r_i) == l_i, 1.0, 0.0).astype(BF16)
                kr_ref[hh, c * tk:(c + 1) * tk, HEAD_LANES:] = onehot
            kmean = jnp.dot(ind, kr_ref[hh, :, :HEAD_LANES], preferred_element_type=F32)
            hi = kmean.astype(BF16)
            km_ref[hh, :nb, :] = hi
            km_ref[hh, nb:, :] = (kmean - hi.astype(F32)).astype(BF16)

    def prologue():
        n_i = lax.broadcasted_iota(jnp.int32, (nb, tq), 0)
        c_i = lax.broadcasted_iota(jnp.int32, (nb, tq), 1)
        is_past = n_i < _block_of(qi * tq + c_i)
        for hh in range(heads):
            qt = _rope_t(q_ref[:, head_lanes[hh]].astype(F32).T, cq_ref[...], sq_ref[...], hd)
            qt = (qt * (hd ** -0.5 * LOG2E)).astype(BF16)
            qt_ref[hh, :HEAD_LANES, :] = qt

            g2 = jnp.dot(km_ref[hh], qt, preferred_element_type=F32)
            gm = jnp.where(is_past, g2[:nb, :] + g2[nb:, :], NEG_INF)
            gate = jnp.where(is_past, NEG_INF, 0.0)
            for _ in range(min(MOBA_TOPK, nb)):
                mx = jnp.max(gm, axis=0, keepdims=True)
                first = jnp.min(jnp.where(gm == mx, n_i, nb), axis=0, keepdims=True)
                sel = n_i == first
                gate = jnp.where(sel, 0.0, gate)
                gm = jnp.where(sel, REMOVED, gm)
            qt_ref[hh, HEAD_LANES:, :] = jnp.concatenate(
                [gate, jnp.zeros((HEAD_LANES - nb, tq), F32)], axis=0).astype(BF16)

    def make_stream(hh):
        def scores(kb):
            k_blk = kr_ref[hh, _chunk_rows(kb, tk), :]
            return jnp.dot(k_blk, qt_ref[hh], preferred_element_type=F32)

        return _Stream(scores, lambda: bias_ref[...], vt_ref.at[hh], s_ref.at[hh],
                       mb_ref.at[hh], m_ref.at[hh], acc_ref.at[hh])

    def epilogue():
        for hh in range(heads):
            denom = acc_ref[hh, HEAD_LANES:HEAD_LANES + 1, :]
            o = acc_ref[hh, :HEAD_LANES, :] * (1.0 / denom)
            o_ref[:, head_lanes[hh]] = o.T.astype(BF16)

    _attend(qi, [make_stream(hh) for hh in range(heads)], prologue, epilogue)


def _moba_attention(proj3, cos_t, sin_t):
    b, seq, _ = proj3.shape
    tq = tk = min(MOBA_TQ, seq)
    heads = ATTN_HEADS_PER_STEP
    bw = heads * HEAD_LANES
    nb = seq // MOBA_BLOCK
    half = cos_t.shape[0]
    qcol = 3 * ATTN_WIDTH // bw
    kcol = 4 * ATTN_WIDTH // bw
    vcol = 5 * ATTN_WIDTH // bw
    tab_q = pl.BlockSpec((half, tq), lambda bi, h, qi: (0, qi))
    tab_k = pl.BlockSpec((half, seq), lambda bi, h, qi: (0, 0))
    return pl.pallas_call(
        functools.partial(_moba_body, heads=heads, seq=seq, tq=tq, tk=tk),
        grid=(b, MOBA_HEADS // heads, seq // tq),
        in_specs=[
            tab_q, tab_q, tab_k, tab_k,
            pl.BlockSpec((tq, tq), lambda bi, h, qi: (0, 0), **ONCE),
            pl.BlockSpec((None, tq, bw), lambda bi, h, qi: (bi, qi, qcol + h)),
            pl.BlockSpec((None, seq, bw), lambda bi, h, qi: (bi, 0, kcol + h)),
            pl.BlockSpec((None, seq, bw), lambda bi, h, qi: (bi, 0, vcol + h)),
        ],
        out_specs=pl.BlockSpec((None, tq, bw), lambda bi, h, qi: (bi, qi, h)),
        out_shape=jax.ShapeDtypeStruct((b, seq, ATTN_WIDTH), BF16),
        scratch_shapes=_head_scratch(heads, seq, tk, tq, 2 * HEAD_LANES) + [
            pltpu.VMEM((heads, 2 * nb, MOBA_HEAD_DIM), BF16),
        ],
        compiler_params=_params("arbitrary", "arbitrary", "arbitrary"),
        name="moba_attn",
    )(cos_t, sin_t, cos_t, sin_t, _moba_diag_bias(tq), proj3, proj3, proj3)


def _merge_body(ya_ref, yb_ref, ga_ref, gb_ref, x_ref, wa_ref, wb_ref, wo_ref, g_ref, o_ref):
    ta = jnp.dot(ya_ref[...], wa_ref[...], preferred_element_type=F32)
    tb = jnp.dot(yb_ref[...], wb_ref[...], preferred_element_type=F32)
    gate_a = jax.nn.sigmoid(ga_ref[...].astype(F32))
    gate_b = jax.nn.sigmoid(gb_ref[...].astype(F32))
    merged = (gate_a * ta + gate_b * tb).astype(BF16)
    y = jnp.dot(merged, wo_ref[...], preferred_element_type=F32)
    o_ref[...] = x_ref[...] + _rms(y, g_ref[...], NORM_EPS)


def _merge(ya, yb, proj, x, w_a, w_b, w_o, g, layer):
    m, d = x.shape
    tm = min(MERGE_TM, m)
    gate_col = 6 * ATTN_WIDTH // d
    return pl.pallas_call(
        _merge_body,
        grid=(m // tm,),
        in_specs=[
            pl.BlockSpec((tm, ATTN_WIDTH), lambda i: (i, 0)),
            pl.BlockSpec((tm, ATTN_WIDTH), lambda i: (i, 0)),
            pl.BlockSpec((tm, d), lambda i: (i, gate_col)),
            pl.BlockSpec((tm, d), lambda i: (i, gate_col + 1)),
            pl.BlockSpec((tm, d), lambda i: (i, 0)),
            pl.BlockSpec((None, ATTN_WIDTH, d), lambda i: (layer, 0, 0), **ONCE),
            pl.BlockSpec((None, ATTN_WIDTH, d), lambda i: (layer, 0, 0), **ONCE),
            pl.BlockSpec((None, d, d), lambda i: (layer, 0, 0), **ONCE),
            pl.BlockSpec((None, 1, d), lambda i: (layer, 0, 0)),
        ],
        out_specs=pl.BlockSpec((tm, d), lambda i: (i, 0)),
        out_shape=jax.ShapeDtypeStruct((m, d), F32),
        compiler_params=_params("arbitrary"),
        name="merge",
    )(ya, yb, proj, proj, x, w_a, w_b, w_o, g)


def _trunk(x, ffn1_pre_g, ffn1_w_gu, ffn1_w_down, ffn1_post_g, mix_pre_g, w_in,
           diff_lq1, diff_lk1, diff_lq2, diff_lk2, diff_subln_g,
           w_branch_diff, w_branch_moba, w_out, mix_post_g,
           ffn2_pre_g, ffn2_w_gu, ffn2_w_down, ffn2_post_g):
    b, seq, d = x.shape
    depth = w_in.shape[0]
    assert (6 * ATTN_WIDTH) % d == 0 and seq % ATTN_TK == 0
    row = lambda p: p[:, None, :]
    bf = lambda w: w.astype(BF16)
    cos_d, sin_d = _rope_tables_t(seq, DIFF_HEAD_DIM)
    cos_m, sin_m = _rope_tables_t(seq, MOBA_HEAD_DIM)
    w1gu, w1d, w2gu, w2d = bf(ffn1_w_gu), bf(ffn1_w_down), bf(ffn2_w_gu), bf(ffn2_w_down)
    win, wa, wb, wo = bf(w_in), bf(w_branch_diff), bf(w_branch_moba), bf(w_out)
    xt = x.reshape(b * seq, d)
    for l in range(depth):
        lambda_init = 0.8 - 0.6 * math.exp(-0.3 * l)
        xt = _ffn(xt, row(ffn1_pre_g), w1gu, w1d, row(ffn1_post_g), l)
        proj = _inproj(xt, row(mix_pre_g), win, l)
        proj3 = proj.reshape(b, seq, proj.shape[1])
        ya = _diff_attention(proj3, cos_d, sin_d, row(diff_lq1), row(diff_lk1), row(diff_lq2),
                             row(diff_lk2), row(diff_subln_g), l, lambda_init)
        yb = _moba_attention(proj3, cos_m, sin_m)
        xt = _merge(ya.reshape(b * seq, ATTN_WIDTH), yb.reshape(b * seq, ATTN_WIDTH), proj, xt,
                    wa, wb, wo, row(mix_post_g), l)
        xt = _ffn(xt, row(ffn2_pre_g), w2gu, w2d, row(ffn2_post_g), l)
    return xt.reshape(b, seq, d)


def kernel(x, ffn1_pre_g, ffn1_w_gu, ffn1_w_down, ffn1_post_g, mix_pre_g, w_in, diff_lq1, diff_lk1, diff_lq2, diff_lk2, diff_subln_g, w_branch_diff, w_branch_moba, w_out, mix_post_g, ffn2_pre_g, ffn2_w_gu, ffn2_w_down, ffn2_post_g):
    return _trunk(x, ffn1_pre_g, ffn1_w_gu, ffn1_w_down, ffn1_post_g, mix_pre_g, w_in,
                  diff_lq1, diff_lk1, diff_lq2, diff_lk2, diff_subln_g,
                  w_branch_diff, w_branch_moba, w_out, mix_post_g,
                  ffn2_pre_g, ffn2_w_gu, ffn2_w_down, ffn2_post_g)
```

```python
import functools
import math
from typing import Any, Callable, NamedTuple

import jax
import jax.numpy as jnp
from jax import lax
from jax.experimental import pallas as pl
from jax.experimental.pallas import tpu as pltpu

F32 = jnp.float32
BF16 = jnp.bfloat16

DIFF_HEADS = 8
DIFF_HEAD_DIM = 64
MOBA_HEADS = 8
MOBA_HEAD_DIM = 128
MOBA_BLOCK = 256
MOBA_TOPK = 3
ATTN_WIDTH = 1024
HEAD_LANES = 128
ROPE_THETA = 500000.0
ROT_FRAC_DIV = 4
NORM_EPS = 1e-6
SUBLN_EPS = 1e-5
NEG_INF = -1e30
REMOVED = -3e38
LOG2E = math.log2(math.e)

SUBLANES = 8
BF16_SUBLANES = 16
ACC_ROWS = HEAD_LANES + BF16_SUBLANES
VMEM_LIMIT_BYTES = 56 * 1024 * 1024

FFN_TM = 1024
FFN_TF = 512
INPROJ_TM = 1024
INPROJ_TN = 2048
MERGE_TM = 512
ATTN_TK = 512
ATTN_COLS = 512
ATTN_HEADS_PER_STEP = 4
DIFF_TQ = ATTN_COLS // 2
MOBA_TQ = ATTN_COLS

ONCE = dict(pipeline_mode=pl.Buffered(1))


def _params(*sem):
    return pltpu.CompilerParams(dimension_semantics=sem, vmem_limit_bytes=VMEM_LIMIT_BYTES)


def _rms(x, g, eps):
    return x * lax.rsqrt(jnp.mean(x * x, axis=-1, keepdims=True) + eps) * g


def _ffn_body(x_ref, pre_g_ref, wg_ref, wu_ref, wd_ref, post_g_ref, o_ref, h_ref, a_ref, *, nf):
    j = pl.program_id(1)

    def up(slot):
        h = h_ref[...]
        g = jnp.dot(h, wg_ref[...], preferred_element_type=F32)
        u = jnp.dot(h, wu_ref[...], preferred_element_type=F32)
        a_ref[slot] = (g * jax.nn.sigmoid(g) * u).astype(BF16)

    def down(slot):
        o_ref[...] += jnp.dot(a_ref[slot], wd_ref[...], preferred_element_type=F32)

    @pl.when(j == 0)
    def _():
        h_ref[...] = _rms(x_ref[...], pre_g_ref[...], NORM_EPS).astype(BF16)
        o_ref[...] = jnp.zeros_like(o_ref)
        up(0)

    middle = (j > 0) & (j < nf)

    @pl.when(middle & (j % 2 == 1))
    def _():
        up(1)
        down(0)

    @pl.when(middle & (j % 2 == 0))
    def _():
        up(0)
        down(1)

    @pl.when(j == nf)
    def _():
        down((nf - 1) % 2)
        o_ref[...] = x_ref[...] + 0.5 * _rms(o_ref[...], post_g_ref[...], NORM_EPS)


def _ffn(x, pre_g, w_gu, w_down, post_g, layer):
    m, d = x.shape
    f = w_down.shape[1]
    tm, tf = min(FFN_TM, m), min(FFN_TF, f)
    nf = f // tf
    up_blk = lambda j: jnp.minimum(j, nf - 1)
    down_blk = lambda j: jnp.maximum(j - 1, 0)
    return pl.pallas_call(
        functools.partial(_ffn_body, nf=nf),
        grid=(m // tm, nf + 1),
        in_specs=[
            pl.BlockSpec((tm, d), lambda i, j: (i, 0)),
            pl.BlockSpec((None, 1, d), lambda i, j: (layer, 0, 0)),
            pl.BlockSpec((None, d, tf), lambda i, j: (layer, 0, up_blk(j))),
            pl.BlockSpec((None, d, tf), lambda i, j: (layer, 0, up_blk(j) + nf)),
            pl.BlockSpec((None, tf, d), lambda i, j: (layer, down_blk(j), 0)),
            pl.BlockSpec((None, 1, d), lambda i, j: (layer, 0, 0)),
        ],
        out_specs=pl.BlockSpec((tm, d), lambda i, j: (i, 0)),
        out_shape=jax.ShapeDtypeStruct((m, d), F32),
        scratch_shapes=[
            pltpu.VMEM((tm, d), BF16),
            pltpu.VMEM((2, tm, tf), BF16),
        ],
        compiler_params=_params("arbitrary", "arbitrary"),
        name="ffn",
    )(x, pre_g, w_gu, w_gu, w_down, post_g)


def _inproj_body(x_ref, g_ref, w_ref, o_ref, h_ref):
    @pl.when(pl.program_id(1) == 0)
    def _():
        h_ref[...] = _rms(x_ref[...], g_ref[...], NORM_EPS).astype(BF16)

    o_ref[...] = jnp.dot(h_ref[...], w_ref[...], preferred_element_type=F32).astype(BF16)


def _inproj(x, g, w_in, layer):
    m, d = x.shape
    width = w_in.shape[2]
    tm = min(INPROJ_TM, m)
    tn = INPROJ_TN if width % INPROJ_TN == 0 else ATTN_WIDTH
    return pl.pallas_call(
        _inproj_body,
        grid=(m // tm, width // tn),
        in_specs=[
            pl.BlockSpec((tm, d), lambda i, j: (i, 0)),
            pl.BlockSpec((None, 1, d), lambda i, j: (layer, 0, 0)),
            pl.BlockSpec((None, d, tn), lambda i, j: (layer, 0, j)),
        ],
        out_specs=pl.BlockSpec((tm, tn), lambda i, j: (i, j)),
        out_shape=jax.ShapeDtypeStruct((m, width), BF16),
        scratch_shapes=[pltpu.VMEM((tm, d), BF16)],
        compiler_params=_params("arbitrary", "arbitrary"),
        name="inproj",
    )(x, g, w_in)


def _rope_tables_t(seq, head_dim):
    rot = head_dim // ROT_FRAC_DIV
    inv = jnp.power(ROPE_THETA, -jnp.arange(0, rot, 2, dtype=F32) / rot)
    ang = inv[:, None] * jnp.arange(seq, dtype=F32)[None, :]
    return jnp.cos(ang), jnp.sin(ang)


def _rope_t(xt, cos, sin, head_dim):
    half = cos.shape[0]
    assert half % SUBLANES == 0
    parts = []
    for base in range(0, xt.shape[0], head_dim):
        x1 = xt[base:base + half]
        x2 = xt[base + half:base + 2 * half]
        parts += [x1 * cos - x2 * sin, x2 * cos + x1 * sin, xt[base + 2 * half:base + head_dim]]
    return jnp.concatenate(parts, axis=0)


class _Stream(NamedTuple):
    scores: Callable[[Any, Any], Any]
    diag_bias: Callable[[int], Any]
    vt: Any
    s: Any
    mb: Any
    m: Any
    acc: Any


def _head_scratch(heads, seq, tk, cols, key_lanes):
    return [
        pltpu.VMEM((heads, key_lanes, cols), BF16),
        pltpu.VMEM((heads, seq, key_lanes), BF16),
        pltpu.VMEM((heads, seq // tk, ACC_ROWS, tk), BF16),
        pltpu.VMEM((heads, 2, tk, cols), F32),
        pltpu.VMEM((heads, 2, 1, cols), F32),
        pltpu.VMEM((heads, 1, cols), F32),
        pltpu.VMEM((heads, ACC_ROWS, cols), F32),
    ]


def _prepare_keys_values(k_ref, v_ref, cos_ref, sin_ref, kr_ref, vt_ref, lanes, seq, tk, head_dim):
    extra = lax.broadcasted_iota(jnp.int32, (ACC_ROWS - HEAD_LANES, tk), 0)
    ones_row = jnp.where(extra == 0, 1.0, 0.0).astype(BF16)
    for c in range(seq // tk):
        rows = slice(c * tk, (c + 1) * tk)
        kt = _rope_t(k_ref[rows, lanes].astype(F32).T, cos_ref[:, rows], sin_ref[:, rows], head_dim)
        kr_ref[rows, :HEAD_LANES] = kt.T.astype(BF16)
        vt_ref[c, :HEAD_LANES, :] = v_ref[rows, lanes].astype(F32).T.astype(BF16)
        vt_ref[c, HEAD_LANES:, :] = ones_row


def _chunk_rows(kb, tk, rows=None):
    start = kb * tk
    return pl.ds(start if isinstance(kb, int) else pl.multiple_of(start, tk), rows or tk)


def _attend(n_past, streams, prologue, epilogue, diag_variants=None):
    def produce(kb, slot, past, rows=None):
        for st in streams:
            sc = st.scores(kb, rows)
            if past:
                st.mb[slot] = jnp.max(sc, axis=0, keepdims=True)
            st.s[slot, :sc.shape[0], :] = sc

    def consume(kb, slot, diag, rows=None):
        for st in streams:
            keys = st.s.shape[1] if rows is None else rows
            sc = st.s[slot, :keys, :]
            vt_blk = st.vt[kb, :, :keys]
            if diag:
                sc = sc + st.diag_bias(keys)
                m_new = jnp.max(sc, axis=0, keepdims=True)
                p = jnp.exp2(sc - m_new).astype(BF16)
                st.acc[...] = jnp.dot(vt_blk, p, preferred_element_type=F32)
            else:
                m_old = st.m[...]
                m_new = jnp.maximum(m_old, st.mb[slot])
                alpha = jnp.exp2(m_old - m_new)
                p = jnp.exp2(sc - m_new).astype(BF16)
                st.acc[...] = alpha * st.acc[...] + jnp.dot(vt_blk, p, preferred_element_type=F32)
            st.m[...] = m_new

    for pred, rows in diag_variants or [(True, None)]:
        @pl.when((n_past == 0) & pred)
        def _(rows=rows):
            prologue()
            produce(0, 0, False, rows)
            consume(0, 0, True, rows)
            epilogue()

        @pl.when((n_past > 0) & pred)
        def _(rows=rows):
            prologue()
            produce(n_past, 0, False, rows)
            produce(0, 1, True)
            consume(n_past, 0, True, rows)

    @pl.when(n_past > 0)
    def _():
        rest = n_past - 1
        pairs = rest // 2

        def pair(t, carry):
            c = 2 * t + 1
            produce(c, 0, True)
            consume(c - 1, 1, False)
            produce(c + 1, 1, True)
            consume(c, 0, False)
            return carry

        lax.fori_loop(0, pairs, pair, 0)
        last = 2 * pairs

        @pl.when(rest % 2 == 1)
        def _():
            produce(last + 1, 0, True)
            consume(last, 1, False)
            consume(last + 1, 0, False)
            epilogue()

        @pl.when(rest % 2 == 0)
        def _():
            consume(last, 1, False)
            epilogue()


def _diff_causal_bias(tq, tk):
    p = lax.broadcasted_iota(jnp.int32, (tk // tq, tk, 2 * tq), 0)
    r = lax.broadcasted_iota(jnp.int32, (tk // tq, tk, 2 * tq), 1)
    c = lax.broadcasted_iota(jnp.int32, (tk // tq, tk, 2 * tq), 2) % tq
    return jnp.where(r <= p * tq + c, 0.0, NEG_INF).astype(F32)


def _diff_body(lq1_ref, lk1_ref, lq2_ref, lk2_ref, sg_ref, cq_ref, sq_ref, ck_ref, sk_ref, bias_ref,
               q_ref, k_ref, v_ref, o_ref,
               qt_ref, kr_ref, vt_ref, s_ref, mb_ref, m_ref, acc_ref,
               *, heads, seq, tq, tk, lambda_init):
    qi = pl.program_id(2)
    d = DIFF_HEAD_DIM
    head_lanes = [slice(hh * HEAD_LANES, (hh + 1) * HEAD_LANES) for hh in range(heads)]

    @pl.when(qi == 0)
    def _():
        for hh in range(heads):
            _prepare_keys_values(k_ref, v_ref, ck_ref, sk_ref, kr_ref.at[hh], vt_ref.at[hh],
                                 head_lanes[hh], seq, tk, d)

    def prologue():
        row = lax.broadcasted_iota(jnp.int32, (2 * d, tq), 0)
        for hh in range(heads):
            qt = _rope_t(q_ref[:, head_lanes[hh]].astype(F32).T, cq_ref[...], sq_ref[...], d)
            qt = qt * (d ** -0.5 * LOG2E)
            qt_ref[hh, :, :tq] = jnp.where(row < d, qt, 0.0).astype(BF16)
            qt_ref[hh, :, tq:] = jnp.where(row >= d, qt, 0.0).astype(BF16)

    def make_stream(hh):
        def scores(kb, rows):
            k_blk = kr_ref[hh, _chunk_rows(kb, tk, rows), :]
            return jnp.dot(k_blk, qt_ref[hh], preferred_element_type=F32)

        return _Stream(scores, lambda rows: bias_ref[qi % (tk // tq), :rows, :], vt_ref.at[hh],
                       s_ref.at[hh], mb_ref.at[hh], m_ref.at[hh], acc_ref.at[hh])

    def epilogue():
        lam = (jnp.exp(jnp.sum(lq1_ref[...] * lk1_ref[...], axis=-1, keepdims=True))
               - jnp.exp(jnp.sum(lq2_ref[...] * lk2_ref[...], axis=-1, keepdims=True))
               + lambda_init)
        for hh in range(heads):
            denom = acc_ref[hh, HEAD_LANES:HEAD_LANES + 1, :]
            acc = acc_ref[hh, :HEAD_LANES, :] * (1.0 / denom)
            o = (acc[:, :tq] - lam * acc[:, tq:]).T
            o_ref[:, head_lanes[hh]] = (
                _rms(o, sg_ref[...], SUBLN_EPS) * (1.0 - lambda_init)).astype(BF16)

    per = tk // tq
    variants = [(qi % per == p, (p + 1) * tq) for p in range(per)]
    _attend((qi * tq) // tk, [make_stream(hh) for hh in range(heads)], prologue, epilogue,
            variants)


def _diff_attention(proj3, cos_t, sin_t, lq1, lk1, lq2, lk2, subln_g, layer, lambda_init):
    b, seq, _ = proj3.shape
    tq, tk = min(DIFF_TQ, seq), min(ATTN_TK, seq)
    heads = ATTN_HEADS_PER_STEP
    bw = heads * HEAD_LANES
    half = cos_t.shape[0]
    kcol = ATTN_WIDTH // bw
    vcol = 2 * ATTN_WIDTH // bw
    lspec = pl.BlockSpec((None, 1, DIFF_HEAD_DIM), lambda bi, h, qi: (layer, 0, 0))
    tab_q = pl.BlockSpec((half, tq), lambda bi, h, qi: (0, qi))
    tab_k = pl.BlockSpec((half, seq), lambda bi, h, qi: (0, 0))
    return pl.pallas_call(
        functools.partial(_diff_body, heads=heads, seq=seq, tq=tq, tk=tk, lambda_init=lambda_init),
        grid=(b, DIFF_HEADS // heads, seq // tq),
        in_specs=[
            lspec, lspec, lspec, lspec,
            pl.BlockSpec((None, 1, HEAD_LANES), lambda bi, h, qi: (layer, 0, 0)),
            tab_q, tab_q, tab_k, tab_k,
            pl.BlockSpec((tk // tq, tk, 2 * tq), lambda bi, h, qi: (0, 0, 0), **ONCE),
            pl.BlockSpec((None, tq, bw), lambda bi, h, qi: (bi, qi, h)),
            pl.BlockSpec((None, seq, bw), lambda bi, h, qi: (bi, 0, kcol + h)),
            pl.BlockSpec((None, seq, bw), lambda bi, h, qi: (bi, 0, vcol + h)),
        ],
        out_specs=pl.BlockSpec((None, tq, bw), lambda bi, h, qi: (bi, qi, h)),
        out_shape=jax.ShapeDtypeStruct((b, seq, ATTN_WIDTH), BF16),
        scratch_shapes=_head_scratch(heads, seq, tk, 2 * tq, HEAD_LANES),
        compiler_params=_params("arbitrary", "arbitrary", "arbitrary"),
        name="diff_attn",
    )(lq1, lk1, lq2, lk2, subln_g, cos_t, sin_t, cos_t, sin_t, _diff_causal_bias(tq, tk),
      proj3, proj3, proj3)


def _block_of(pos):
    return jnp.right_shift(pos, MOBA_BLOCK.bit_length() - 1)


def _moba_diag_bias(tq):
    r = lax.broadcasted_iota(jnp.int32, (tq, tq), 0)
    c = lax.broadcasted_iota(jnp.int32, (tq, tq), 1)
    r_blk, c_blk = _block_of(r), _block_of(c)
    keep = ((r <= c) & (r_blk == c_blk)) | (r_blk < c_blk)
    return jnp.where(keep, 0.0, NEG_INF).astype(F32)


def _moba_body(cq_ref, sq_ref, ck_ref, sk_ref, bias_ref, q_ref, k_ref, v_ref, o_ref,
               qt_ref, kr_ref, vt_ref, s_ref, mb_ref, m_ref, acc_ref, km_ref,
               *, heads, seq, tq, tk):
    qi = pl.program_id(2)
    blk = MOBA_BLOCK
    hd = MOBA_HEAD_DIM
    nb = seq // blk
    assert nb <= HEAD_LANES
    head_lanes = [slice(hh * HEAD_LANES, (hh + 1) * HEAD_LANES) for hh in range(heads)]

    @pl.when(qi == 0)
    def _():
        n_i = lax.broadcasted_iota(jnp.int32, (nb, seq), 0)
        s_i = lax.broadcasted_iota(jnp.int32, (nb, seq), 1)
        ind = jnp.where(_block_of(s_i) == n_i, 1.0 / blk, 0.0).astype(BF16)
        r_i = lax.broadcasted_iota(jnp.int32, (tk, HEAD_LANES), 0)
        l_i = lax.broadcasted_iota(jnp.int32, (tk, HEAD_LANES), 1)
        for hh in range(heads):
            _prepare_keys_values(k_ref, v_ref, ck_ref, sk_ref, kr_ref.at[hh], vt_ref.at[hh],
                                 head_lanes[hh], seq, tk, hd)
            for c in range(seq // tk):
                onehot = jnp.where(_block_of(c * tk + r_i) == l_i, 1.0, 0.0).astype(BF16)
                kr_ref[hh, c * tk:(c + 1) * tk, HEAD_LANES:] = onehot
            kmean = jnp.dot(ind, kr_ref[hh, :, :HEAD_LANES], preferred_element_type=F32)
            hi = kmean.astype(BF16)
            km_ref[hh, :nb, :] = hi
            km_ref[hh, nb:, :] = (kmean - hi.astype(F32)).astype(BF16)

    def prologue():
        n_i = lax.broadcasted_iota(jnp.int32, (nb, tq), 0)
        c_i = lax.broadcasted_iota(jnp.int32, (nb, tq), 1)
        is_past = n_i < _block_of(qi * tq + c_i)
        for hh in range(heads):
            qt = _rope_t(q_ref[:, head_lanes[hh]].astype(F32).T, cq_ref[...], sq_ref[...], hd)
            qt = (qt * (hd ** -0.5 * LOG2E)).astype(BF16)
            qt_ref[hh, :HEAD_LANES, :] = qt

            g2 = jnp.dot(km_ref[hh], qt, preferred_element_type=F32)
            gm = jnp.where(is_past, g2[:nb, :] + g2[nb:, :], NEG_INF)
            gate = jnp.where(is_past, NEG_INF, 0.0)
            for _ in range(min(MOBA_TOPK, nb)):
                mx = jnp.max(gm, axis=0, keepdims=True)
                first = jnp.min(jnp.where(gm == mx, n_i, nb), axis=0, keepdims=True)
                sel = n_i == first
                gate = jnp.where(sel, 0.0, gate)
                gm = jnp.where(sel, REMOVED, gm)
            qt_ref[hh, HEAD_LANES:, :] = jnp.concatenate(
                [gate, jnp.zeros((HEAD_LANES - nb, tq), F32)], axis=0).astype(BF16)

    def make_stream(hh):
        def scores(kb, rows):
            k_blk = kr_ref[hh, _chunk_rows(kb, tk, rows), :]
            return jnp.dot(k_blk, qt_ref[hh], preferred_element_type=F32)

        return _Stream(scores, lambda rows: bias_ref[:rows, :], vt_ref.at[hh], s_ref.at[hh],
                       mb_ref.at[hh], m_ref.at[hh], acc_ref.at[hh])

    def epilogue():
        for hh in range(heads):
            denom = acc_ref[hh, HEAD_LANES:HEAD_LANES + 1, :]
            o = acc_ref[hh, :HEAD_LANES, :] * (1.0 / denom)
            o_ref[:, head_lanes[hh]] = o.T.astype(BF16)

    _attend(qi, [make_stream(hh) for hh in range(heads)], prologue, epilogue)


def _moba_attention(proj3, cos_t, sin_t):
    b, seq, _ = proj3.shape
    tq = tk = min(MOBA_TQ, seq)
    heads = ATTN_HEADS_PER_STEP
    bw = heads * HEAD_LANES
    nb = seq // MOBA_BLOCK
    half = cos_t.shape[0]
    qcol = 3 * ATTN_WIDTH // bw
    kcol = 4 * ATTN_WIDTH // bw
    vcol = 5 * ATTN_WIDTH // bw
    tab_q = pl.BlockSpec((half, tq), lambda bi, h, qi: (0, qi))
    tab_k = pl.BlockSpec((half, seq), lambda bi, h, qi: (0, 0))
    return pl.pallas_call(
        functools.partial(_moba_body, heads=heads, seq=seq, tq=tq, tk=tk),
        grid=(b, MOBA_HEADS // heads, seq // tq),
        in_specs=[
            tab_q, tab_q, tab_k, tab_k,
            pl.BlockSpec((tq, tq), lambda bi, h, qi: (0, 0), **ONCE),
            pl.BlockSpec((None, tq, bw), lambda bi, h, qi: (bi, qi, qcol + h)),
            pl.BlockSpec((None, seq, bw), lambda bi, h, qi: (bi, 0, kcol + h)),
            pl.BlockSpec((None, seq, bw), lambda bi, h, qi: (bi, 0, vcol + h)),
        ],
        out_specs=pl.BlockSpec((None, tq, bw), lambda bi, h, qi: (bi, qi, h)),
        out_shape=jax.ShapeDtypeStruct((b, seq, ATTN_WIDTH), BF16),
        scratch_shapes=_head_scratch(heads, seq, tk, tq, 2 * HEAD_LANES) + [
            pltpu.VMEM((heads, 2 * nb, MOBA_HEAD_DIM), BF16),
        ],
        compiler_params=_params("arbitrary", "arbitrary", "arbitrary"),
        name="moba_attn",
    )(cos_t, sin_t, cos_t, sin_t, _moba_diag_bias(tq), proj3, proj3, proj3)


def _merge_body(ya_ref, yb_ref, ga_ref, gb_ref, x_ref, wa_ref, wb_ref, wo_ref, g_ref, o_ref):
    ta = jnp.dot(ya_ref[...], wa_ref[...], preferred_element_type=F32)
    tb = jnp.dot(yb_ref[...], wb_ref[...], preferred_element_type=F32)
    gate_a = jax.nn.sigmoid(ga_ref[...].astype(F32))
    gate_b = jax.nn.sigmoid(gb_ref[...].astype(F32))
    merged = (gate_a * ta + gate_b * tb).astype(BF16)
    y = jnp.dot(merged, wo_ref[...], preferred_element_type=F32)
    o_ref[...] = x_ref[...] + _rms(y, g_ref[...], NORM_EPS)


def _merge(ya, yb, proj, x, w_a, w_b, w_o, g, layer):
    m, d = x.shape
    tm = min(MERGE_TM, m)
    gate_col = 6 * ATTN_WIDTH // d
    return pl.pallas_call(
        _merge_body,
        grid=(m // tm,),
        in_specs=[
            pl.BlockSpec((tm, ATTN_WIDTH), lambda i: (i, 0)),
            pl.BlockSpec((tm, ATTN_WIDTH), lambda i: (i, 0)),
            pl.BlockSpec((tm, d), lambda i: (i, gate_col)),
            pl.BlockSpec((tm, d), lambda i: (i, gate_col + 1)),
            pl.BlockSpec((tm, d), lambda i: (i, 0)),
            pl.BlockSpec((None, ATTN_WIDTH, d), lambda i: (layer, 0, 0), **ONCE),
            pl.BlockSpec((None, ATTN_WIDTH, d), lambda i: (layer, 0, 0), **ONCE),
            pl.BlockSpec((None, d, d), lambda i: (layer, 0, 0), **ONCE),
            pl.BlockSpec((None, 1, d), lambda i: (layer, 0, 0)),
        ],
        out_specs=pl.BlockSpec((tm, d), lambda i: (i, 0)),
        out_shape=jax.ShapeDtypeStruct((m, d), F32),
        compiler_params=_params("arbitrary"),
        name="merge",
    )(ya, yb, proj, proj, x, w_a, w_b, w_o, g)


def _trunk(x, ffn1_pre_g, ffn1_w_gu, ffn1_w_down, ffn1_post_g, mix_pre_g, w_in,
           diff_lq1, diff_lk1, diff_lq2, diff_lk2, diff_subln_g,
           w_branch_diff, w_branch_moba, w_out, mix_post_g,
           ffn2_pre_g, ffn2_w_gu, ffn2_w_down, ffn2_post_g):
    b, seq, d = x.shape
    depth = w_in.shape[0]
    assert (6 * ATTN_WIDTH) % d == 0 and seq % ATTN_TK == 0
    row = lambda p: p[:, None, :]
    bf = lambda w: w.astype(BF16)
    cos_d, sin_d = _rope_tables_t(seq, DIFF_HEAD_DIM)
    cos_m, sin_m = _rope_tables_t(seq, MOBA_HEAD_DIM)
    w1gu, w1d, w2gu, w2d = bf(ffn1_w_gu), bf(ffn1_w_down), bf(ffn2_w_gu), bf(ffn2_w_down)
    win, wa, wb, wo = bf(w_in), bf(w_branch_diff), bf(w_branch_moba), bf(w_out)
    xt = x.reshape(b * seq, d)
    for l in range(depth):
        lambda_init = 0.8 - 0.6 * math.exp(-0.3 * l)
        xt = _ffn(xt, row(ffn1_pre_g), w1gu, w1d, row(ffn1_post_g), l)
        proj = _inproj(xt, row(mix_pre_g), win, l)
        proj3 = proj.reshape(b, seq, proj.shape[1])
        ya = _diff_attention(proj3, cos_d, sin_d, row(diff_lq1), row(diff_lk1), row(diff_lq2),
                             row(diff_lk2), row(diff_subln_g), l, lambda_init)
        yb = _moba_attention(proj3, cos_m, sin_m)
        xt = _merge(ya.reshape(b * seq, ATTN_WIDTH), yb.reshape(b * seq, ATTN_WIDTH), proj, xt,
                    wa, wb, wo, row(mix_post_g), l)
        xt = _ffn(xt, row(ffn2_pre_g), w2gu, w2d, row(ffn2_post_g), l)
    return xt.reshape(b, seq, d)


def kernel(x, ffn1_pre_g, ffn1_w_gu, ffn1_w_down, ffn1_post_g, mix_pre_g, w_in, diff_lq1, diff_lk1, diff_lq2, diff_lk2, diff_subln_g, w_branch_diff, w_branch_moba, w_out, mix_post_g, ffn2_pre_g, ffn2_w_gu, ffn2_w_down, ffn2_post_g):
    return _trunk(x, ffn1_pre_g, ffn1_w_gu, ffn1_w_down, ffn1_post_g, mix_pre_g, w_in,
                  diff_lq1, diff_lk1, diff_lq2, diff_lk2, diff_subln_g,
                  w_branch_diff, w_branch_moba, w_out, mix_post_g,
                  ffn2_pre_g, ffn2_w_gu, ffn2_w_down, ffn2_post_g)
```

```python
import functools
import math
from typing import Any, Callable, NamedTuple

import jax
import jax.numpy as jnp
from jax import lax
from jax.experimental import pallas as pl
from jax.experimental.pallas import tpu as pltpu

F32 = jnp.float32
BF16 = jnp.bfloat16

DIFF_HEADS = 8
DIFF_HEAD_DIM = 64
MOBA_HEADS = 8
MOBA_HEAD_DIM = 128
MOBA_BLOCK = 256
MOBA_TOPK = 3
ATTN_WIDTH = 1024
HEAD_LANES = 128
ROPE_THETA = 500000.0
ROT_FRAC_DIV = 4
NORM_EPS = 1e-6
SUBLN_EPS = 1e-5
NEG_INF = -1e30
REMOVED = -3e38
LOG2E = math.log2(math.e)

SUBLANES = 8
BF16_SUBLANES = 16
ACC_ROWS = HEAD_LANES + BF16_SUBLANES
VMEM_LIMIT_BYTES = 56 * 1024 * 1024

FFN_TM = 1024
FFN_TF = 512
INPROJ_TM = 1024
INPROJ_TN = 2048
MERGE_TM = 512
ATTN_TK = 512
ATTN_COLS = 512
ATTN_HEADS_PER_STEP = 4
DIFF_TQ = ATTN_COLS // 2
MOBA_TQ = ATTN_COLS

ONCE = dict(pipeline_mode=pl.Buffered(1))


def _params(*sem):
    return pltpu.CompilerParams(dimension_semantics=sem, vmem_limit_bytes=VMEM_LIMIT_BYTES)


def _rms(x, g, eps):
    return x * lax.rsqrt(jnp.mean(x * x, axis=-1, keepdims=True) + eps) * g


def _ffn_body(x_ref, pre_g_ref, wg_ref, wu_ref, wd_ref, post_g_ref, o_ref, h_ref, a_ref, *, nf):
    j = pl.program_id(1)

    def up(slot):
        h = h_ref[...]
        half = wg_ref.shape[1] // 2
        for cols in (slice(0, half), slice(half, 2 * half)):
            g = jnp.dot(h, wg_ref[:, cols], preferred_element_type=F32)
            u = jnp.dot(h, wu_ref[:, cols], preferred_element_type=F32)
            a_ref[slot, :, cols] = (g * jax.nn.sigmoid(g) * u).astype(BF16)

    def down(slot):
        o_ref[...] += jnp.dot(a_ref[slot], wd_ref[...], preferred_element_type=F32)

    @pl.when(j == 0)
    def _():
        h_ref[...] = _rms(x_ref[...], pre_g_ref[...], NORM_EPS).astype(BF16)
        o_ref[...] = jnp.zeros_like(o_ref)
        up(0)

    middle = (j > 0) & (j < nf)

    @pl.when(middle & (j % 2 == 1))
    def _():
        up(1)
        down(0)

    @pl.when(middle & (j % 2 == 0))
    def _():
        up(0)
        down(1)

    @pl.when(j == nf)
    def _():
        down((nf - 1) % 2)
        o_ref[...] = x_ref[...] + 0.5 * _rms(o_ref[...], post_g_ref[...], NORM_EPS)


def _ffn(x, pre_g, w_gu, w_down, post_g, layer):
    m, d = x.shape
    f = w_down.shape[1]
    tm, tf = min(FFN_TM, m), min(FFN_TF, f)
    nf = f // tf
    up_blk = lambda j: jnp.minimum(j, nf - 1)
    down_blk = lambda j: jnp.maximum(j - 1, 0)
    return pl.pallas_call(
        functools.partial(_ffn_body, nf=nf),
        grid=(m // tm, nf + 1),
        in_specs=[
            pl.BlockSpec((tm, d), lambda i, j: (i, 0)),
            pl.BlockSpec((None, 1, d), lambda i, j: (layer, 0, 0)),
            pl.BlockSpec((None, d, tf), lambda i, j: (layer, 0, up_blk(j))),
            pl.BlockSpec((None, d, tf), lambda i, j: (layer, 0, up_blk(j) + nf)),
            pl.BlockSpec((None, tf, d), lambda i, j: (layer, down_blk(j), 0)),
            pl.BlockSpec((None, 1, d), lambda i, j: (layer, 0, 0)),
        ],
        out_specs=pl.BlockSpec((tm, d), lambda i, j: (i, 0)),
        out_shape=jax.ShapeDtypeStruct((m, d), F32),
        scratch_shapes=[
            pltpu.VMEM((tm, d), BF16),
            pltpu.VMEM((2, tm, tf), BF16),
        ],
        compiler_params=_params("arbitrary", "arbitrary"),
        name="ffn",
    )(x, pre_g, w_gu, w_gu, w_down, post_g)


def _inproj_body(x_ref, g_ref, w_ref, o_ref, h_ref):
    @pl.when(pl.program_id(1) == 0)
    def _():
        h_ref[...] = _rms(x_ref[...], g_ref[...], NORM_EPS).astype(BF16)

    o_ref[...] = jnp.dot(h_ref[...], w_ref[...], preferred_element_type=F32).astype(BF16)


def _inproj(x, g, w_in, layer):
    m, d = x.shape
    width = w_in.shape[2]
    tm = min(INPROJ_TM, m)
    tn = INPROJ_TN if width % INPROJ_TN == 0 else ATTN_WIDTH
    return pl.pallas_call(
        _inproj_body,
        grid=(m // tm, width // tn),
        in_specs=[
            pl.BlockSpec((tm, d), lambda i, j: (i, 0)),
            pl.BlockSpec((None, 1, d), lambda i, j: (layer, 0, 0)),
            pl.BlockSpec((None, d, tn), lambda i, j: (layer, 0, j)),
        ],
        out_specs=pl.BlockSpec((tm, tn), lambda i, j: (i, j)),
        out_shape=jax.ShapeDtypeStruct((m, width), BF16),
        scratch_shapes=[pltpu.VMEM((tm, d), BF16)],
        compiler_params=_params("arbitrary", "arbitrary"),
        name="inproj",
    )(x, g, w_in)


def _rope_tables_t(seq, head_dim):
    rot = head_dim // ROT_FRAC_DIV
    inv = jnp.power(ROPE_THETA, -jnp.arange(0, rot, 2, dtype=F32) / rot)
    ang = inv[:, None] * jnp.arange(seq, dtype=F32)[None, :]
    return jnp.cos(ang), jnp.sin(ang)


def _rope_t(xt, cos, sin, head_dim):
    half = cos.shape[0]
    assert half % SUBLANES == 0
    parts = []
    for base in range(0, xt.shape[0], head_dim):
        x1 = xt[base:base + half]
        x2 = xt[base + half:base + 2 * half]
        parts += [x1 * cos - x2 * sin, x2 * cos + x1 * sin, xt[base + 2 * half:base + head_dim]]
    return jnp.concatenate(parts, axis=0)


class _Stream(NamedTuple):
    scores: Callable[[Any, Any], Any]
    diag_bias: Callable[[int], Any]
    vt: Any
    s: Any
    mb: Any
    m: Any
    acc: Any


def _head_scratch(heads, seq, tk, cols, key_lanes):
    return [
        pltpu.VMEM((heads, key_lanes, cols), BF16),
        pltpu.VMEM((heads, seq, key_lanes), BF16),
        pltpu.VMEM((heads, seq // tk, ACC_ROWS, tk), BF16),
        pltpu.VMEM((heads, 2, tk, cols), F32),
        pltpu.VMEM((heads, 2, 1, cols), F32),
        pltpu.VMEM((heads, 1, cols), F32),
        pltpu.VMEM((heads, ACC_ROWS, cols), F32),
    ]


def _prepare_keys_values(k_ref, v_ref, cos_ref, sin_ref, kr_ref, vt_ref, lanes, seq, tk, head_dim):
    extra = lax.broadcasted_iota(jnp.int32, (ACC_ROWS - HEAD_LANES, tk), 0)
    ones_row = jnp.where(extra == 0, 1.0, 0.0).astype(BF16)
    for c in range(seq // tk):
        rows = slice(c * tk, (c + 1) * tk)
        kt = _rope_t(k_ref[rows, lanes].astype(F32).T, cos_ref[:, rows], sin_ref[:, rows], head_dim)
        kr_ref[rows, :HEAD_LANES] = kt.T.astype(BF16)
        vt_ref[c, :HEAD_LANES, :] = v_ref[rows, lanes].astype(F32).T.astype(BF16)
        vt_ref[c, HEAD_LANES:, :] = ones_row


def _chunk_rows(kb, tk, rows=None):
    start = kb * tk
    return pl.ds(start if isinstance(kb, int) else pl.multiple_of(start, tk), rows or tk)


def _attend(n_past, streams, prologue, epilogue, diag_variants=None):
    def produce(kb, slot, past, rows=None):
        for st in streams:
            sc = st.scores(kb, rows)
            if past:
                st.mb[slot] = jnp.max(sc, axis=0, keepdims=True)
            st.s[slot, :sc.shape[0], :] = sc

    def consume(kb, slot, diag, rows=None):
        for st in streams:
            keys = st.s.shape[1] if rows is None else rows
            sc = st.s[slot, :keys, :]
            vt_blk = st.vt[kb, :, :keys]
            if diag:
                sc = sc + st.diag_bias(keys)
                m_new = jnp.max(sc, axis=0, keepdims=True)
                p = jnp.exp2(sc - m_new).astype(BF16)
                st.acc[...] = jnp.dot(vt_blk, p, preferred_element_type=F32)
            else:
                m_old = st.m[...]
                m_new = jnp.maximum(m_old, st.mb[slot])
                alpha = jnp.exp2(m_old - m_new)
                p = jnp.exp2(sc - m_new).astype(BF16)
                st.acc[...] = alpha * st.acc[...] + jnp.dot(vt_blk, p, preferred_element_type=F32)
            st.m[...] = m_new

    for pred, rows in diag_variants or [(True, None)]:
        @pl.when((n_past == 0) & pred)
        def _(rows=rows):
            prologue()
            produce(0, 0, False, rows)
            consume(0, 0, True, rows)
            epilogue()

        @pl.when((n_past > 0) & pred)
        def _(rows=rows):
            prologue()
            produce(n_past, 0, False, rows)
            produce(0, 1, True)
            consume(n_past, 0, True, rows)

    @pl.when(n_past > 0)
    def _():
        rest = n_past - 1
        pairs = rest // 2

        def pair(t, carry):
            c = 2 * t + 1
            produce(c, 0, True)
            consume(c - 1, 1, False)
            produce(c + 1, 1, True)
            consume(c, 0, False)
            return carry

        lax.fori_loop(0, pairs, pair, 0)
        last = 2 * pairs

        @pl.when(rest % 2 == 1)
        def _():
            produce(last + 1, 0, True)
            consume(last, 1, False)
            consume(last + 1, 0, False)
            epilogue()

        @pl.when(rest % 2 == 0)
        def _():
            consume(last, 1, False)
            epilogue()


def _diff_causal_bias(tq, tk):
    p = lax.broadcasted_iota(jnp.int32, (tk // tq, tk, 2 * tq), 0)
    r = lax.broadcasted_iota(jnp.int32, (tk // tq, tk, 2 * tq), 1)
    c = lax.broadcasted_iota(jnp.int32, (tk // tq, tk, 2 * tq), 2) % tq
    return jnp.where(r <= p * tq + c, 0.0, NEG_INF).astype(F32)


def _diff_body(lq1_ref, lk1_ref, lq2_ref, lk2_ref, sg_ref, cq_ref, sq_ref, ck_ref, sk_ref, bias_ref,
               q_ref, k_ref, v_ref, o_ref,
               qt_ref, kr_ref, vt_ref, s_ref, mb_ref, m_ref, acc_ref,
               *, heads, seq, tq, tk, lambda_init):
    qi = pl.program_id(2)
    d = DIFF_HEAD_DIM
    head_lanes = [slice(hh * HEAD_LANES, (hh + 1) * HEAD_LANES) for hh in range(heads)]

    @pl.when(qi == 0)
    def _():
        for hh in range(heads):
            _prepare_keys_values(k_ref, v_ref, ck_ref, sk_ref, kr_ref.at[hh], vt_ref.at[hh],
                                 head_lanes[hh], seq, tk, d)

    def prologue():
        row = lax.broadcasted_iota(jnp.int32, (2 * d, tq), 0)
        for hh in range(heads):
            qt = _rope_t(q_ref[:, head_lanes[hh]].astype(F32).T, cq_ref[...], sq_ref[...], d)
            qt = qt * (d ** -0.5 * LOG2E)
            qt_ref[hh, :, :tq] = jnp.where(row < d, qt, 0.0).astype(BF16)
            qt_ref[hh, :, tq:] = jnp.where(row >= d, qt, 0.0).astype(BF16)

    def make_stream(hh):
        def scores(kb, rows):
            k_blk = kr_ref[hh, _chunk_rows(kb, tk, rows), :]
            return jnp.dot(k_blk, qt_ref[hh], preferred_element_type=F32)

        return _Stream(scores, lambda rows: bias_ref[qi % (tk // tq), :rows, :], vt_ref.at[hh],
                       s_ref.at[hh], mb_ref.at[hh], m_ref.at[hh], acc_ref.at[hh])

    def epilogue():
        lam = (jnp.exp(jnp.sum(lq1_ref[...] * lk1_ref[...], axis=-1, keepdims=True))
               - jnp.exp(jnp.sum(lq2_ref[...] * lk2_ref[...], axis=-1, keepdims=True))
               + lambda_init)
        for hh in range(heads):
            denom = acc_ref[hh, HEAD_LANES:HEAD_LANES + 1, :]
            acc = acc_ref[hh, :HEAD_LANES, :] * (1.0 / denom)
            o = (acc[:, :tq] - lam * acc[:, tq:]).T
            o_ref[:, head_lanes[hh]] = (
                _rms(o, sg_ref[...], SUBLN_EPS) * (1.0 - lambda_init)).astype(BF16)

    per = tk // tq
    variants = [(qi % per == p, (p + 1) * tq) for p in range(per)]
    _attend((qi * tq) // tk, [make_stream(hh) for hh in range(heads)], prologue, epilogue,
            variants)


def _diff_attention(proj3, cos_t, sin_t, lq1, lk1, lq2, lk2, subln_g, layer, lambda_init):
    b, seq, _ = proj3.shape
    tq, tk = min(DIFF_TQ, seq), min(ATTN_TK, seq)
    heads = ATTN_HEADS_PER_STEP
    bw = heads * HEAD_LANES
    half = cos_t.shape[0]
    kcol = ATTN_WIDTH // bw
    vcol = 2 * ATTN_WIDTH // bw
    lspec = pl.BlockSpec((None, 1, DIFF_HEAD_DIM), lambda bi, h, qi: (layer, 0, 0))
    tab_q = pl.BlockSpec((half, tq), lambda bi, h, qi: (0, qi))
    tab_k = pl.BlockSpec((half, seq), lambda bi, h, qi: (0, 0))
    return pl.pallas_call(
        functools.partial(_diff_body, heads=heads, seq=seq, tq=tq, tk=tk, lambda_init=lambda_init),
        grid=(b, DIFF_HEADS // heads, seq // tq),
        in_specs=[
            lspec, lspec, lspec, lspec,
            pl.BlockSpec((None, 1, HEAD_LANES), lambda bi, h, qi: (layer, 0, 0)),
            tab_q, tab_q, tab_k, tab_k,
            pl.BlockSpec((tk // tq, tk, 2 * tq), lambda bi, h, qi: (0, 0, 0), **ONCE),
            pl.BlockSpec((None, tq, bw), lambda bi, h, qi: (bi, qi, h)),
            pl.BlockSpec((None, seq, bw), lambda bi, h, qi: (bi, 0, kcol + h)),
            pl.BlockSpec((None, seq, bw), lambda bi, h, qi: (bi, 0, vcol + h)),
        ],
        out_specs=pl.BlockSpec((None, tq, bw), lambda bi, h, qi: (bi, qi, h)),
        out_shape=jax.ShapeDtypeStruct((b, seq, ATTN_WIDTH), BF16),
        scratch_shapes=_head_scratch(heads, seq, tk, 2 * tq, HEAD_LANES),
        compiler_params=_params("arbitrary", "arbitrary", "arbitrary"),
        name="diff_attn",
    )(lq1, lk1, lq2, lk2, subln_g, cos_t, sin_t, cos_t, sin_t, _diff_causal_bias(tq, tk),
      proj3, proj3, proj3)


def _block_of(pos):
    return jnp.right_shift(pos, MOBA_BLOCK.bit_length() - 1)


def _moba_diag_bias(tq):
    r = lax.broadcasted_iota(jnp.int32, (tq, tq), 0)
    c = lax.broadcasted_iota(jnp.int32, (tq, tq), 1)
    r_blk, c_blk = _block_of(r), _block_of(c)
    keep = ((r <= c) & (r_blk == c_blk)) | (r_blk < c_blk)
    return jnp.where(keep, 0.0, NEG_INF).astype(F32)


def _moba_body(cq_ref, sq_ref, ck_ref, sk_ref, bias_ref, q_ref, k_ref, v_ref, o_ref,
               qt_ref, kr_ref, vt_ref, s_ref, mb_ref, m_ref, acc_ref, km_ref,
               *, heads, seq, tq, tk):
    qi = pl.program_id(2)
    blk = MOBA_BLOCK
    hd = MOBA_HEAD_DIM
    nb = seq // blk
    assert nb <= HEAD_LANES
    head_lanes = [slice(hh * HEAD_LANES, (hh + 1) * HEAD_LANES) for hh in range(heads)]

    @pl.when(qi == 0)
    def _():
        n_i = lax.broadcasted_iota(jnp.int32, (nb, seq), 0)
        s_i = lax.broadcasted_iota(jnp.int32, (nb, seq), 1)
        ind = jnp.where(_block_of(s_i) == n_i, 1.0 / blk, 0.0).astype(BF16)
        r_i = lax.broadcasted_iota(jnp.int32, (tk, HEAD_LANES), 0)
        l_i = lax.broadcasted_iota(jnp.int32, (tk, HEAD_LANES), 1)
        for hh in range(heads):
            _prepare_keys_values(k_ref, v_ref, ck_ref, sk_ref, kr_ref.at[hh], vt_ref.at[hh],
                                 head_lanes[hh], seq, tk, hd)
            for c in range(seq // tk):
                onehot = jnp.where(_block_of(c * tk + r_i) == l_i, 1.0, 0.0).astype(BF16)
                kr_ref[hh, c * tk:(c + 1) * tk, HEAD_LANES:] = onehot
            kmean = jnp.dot(ind, kr_ref[hh, :, :HEAD_LANES], preferred_element_type=F32)
            hi = kmean.astype(BF16)
            km_ref[hh, :nb, :] = hi
            km_ref[hh, nb:, :] = (kmean - hi.astype(F32)).astype(BF16)

    def prologue():
        n_i = lax.broadcasted_iota(jnp.int32, (nb, tq), 0)
        c_i = lax.broadcasted_iota(jnp.int32, (nb, tq), 1)
        is_past = n_i < _block_of(qi * tq + c_i)
        for hh in range(heads):
            qt = _rope_t(q_ref[:, head_lanes[hh]].astype(F32).T, cq_ref[...], sq_ref[...], hd)
            qt = (qt * (hd ** -0.5 * LOG2E)).astype(BF16)
            qt_ref[hh, :HEAD_LANES, :] = qt

            g2 = jnp.dot(km_ref[hh], qt, preferred_element_type=F32)
            gm = jnp.where(is_past, g2[:nb, :] + g2[nb:, :], NEG_INF)
            gate = jnp.where(is_past, NEG_INF, 0.0)
            for _ in range(min(MOBA_TOPK, nb)):
                mx = jnp.max(gm, axis=0, keepdims=True)
                first = jnp.min(jnp.where(gm == mx, n_i, nb), axis=0, keepdims=True)
                sel = n_i == first
                gate = jnp.where(sel, 0.0, gate)
                gm = jnp.where(sel, REMOVED, gm)
            qt_ref[hh, HEAD_LANES:, :] = jnp.concatenate(
                [gate, jnp.zeros((HEAD_LANES - nb, tq), F32)], axis=0).astype(BF16)

    def make_stream(hh):
        def scores(kb, rows):
            k_blk = kr_ref[hh, _chunk_rows(kb, tk, rows), :]
            return jnp.dot(k_blk, qt_ref[hh], preferred_element_type=F32)

        return _Stream(scores, lambda rows: bias_ref[:rows, :], vt_ref.at[hh], s_ref.at[hh],
                       mb_ref.at[hh], m_ref.at[hh], acc_ref.at[hh])

    def epilogue():
        for hh in range(heads):
            denom = acc_ref[hh, HEAD_LANES:HEAD_LANES + 1, :]
            o = acc_ref[hh, :HEAD_LANES, :] * (1.0 / denom)
            o_ref[:, head_lanes[hh]] = o.T.astype(BF16)

    _attend(qi, [make_stream(hh) for hh in range(heads)], prologue, epilogue)


def _moba_attention(proj3, cos_t, sin_t):
    b, seq, _ = proj3.shape
    tq = tk = min(MOBA_TQ, seq)
    heads = ATTN_HEADS_PER_STEP
    bw = heads * HEAD_LANES
    nb = seq // MOBA_BLOCK
    half = cos_t.shape[0]
    qcol = 3 * ATTN_WIDTH // bw
    kcol = 4 * ATTN_WIDTH // bw
    vcol = 5 * ATTN_WIDTH // bw
    tab_q = pl.BlockSpec((half, tq), lambda bi, h, qi: (0, qi))
    tab_k = pl.BlockSpec((half, seq), lambda bi, h, qi: (0, 0))
    return pl.pallas_call(
        functools.partial(_moba_body, heads=heads, seq=seq, tq=tq, tk=tk),
        grid=(b, MOBA_HEADS // heads, seq // tq),
        in_specs=[
            tab_q, tab_q, tab_k, tab_k,
            pl.BlockSpec((tq, tq), lambda bi, h, qi: (0, 0), **ONCE),
            pl.BlockSpec((None, tq, bw), lambda bi, h, qi: (bi, qi, qcol + h)),
            pl.BlockSpec((None, seq, bw), lambda bi, h, qi: (bi, 0, kcol + h)),
            pl.BlockSpec((None, seq, bw), lambda bi, h, qi: (bi, 0, vcol + h)),
        ],
        out_specs=pl.BlockSpec((None, tq, bw), lambda bi, h, qi: (bi, qi, h)),
        out_shape=jax.ShapeDtypeStruct((b, seq, ATTN_WIDTH), BF16),
        scratch_shapes=_head_scratch(heads, seq, tk, tq, 2 * HEAD_LANES) + [
            pltpu.VMEM((heads, 2 * nb, MOBA_HEAD_DIM), BF16),
        ],
        compiler_params=_params("arbitrary", "arbitrary", "arbitrary"),
        name="moba_attn",
    )(cos_t, sin_t, cos_t, sin_t, _moba_diag_bias(tq), proj3, proj3, proj3)


def _merge_body(ya_ref, yb_ref, ga_ref, gb_ref, x_ref, wa_ref, wb_ref, wo_ref, g_ref, o_ref):
    ta = jnp.dot(ya_ref[...], wa_ref[...], preferred_element_type=F32)
    tb = jnp.dot(yb_ref[...], wb_ref[...], preferred_element_type=F32)
    gate_a = jax.nn.sigmoid(ga_ref[...].astype(F32))
    gate_b = jax.nn.sigmoid(gb_ref[...].astype(F32))
    merged = (gate_a * ta + gate_b * tb).astype(BF16)
    y = jnp.dot(merged, wo_ref[...], preferred_element_type=F32)
    o_ref[...] = x_ref[...] + _rms(y, g_ref[...], NORM_EPS)


def _merge(ya, yb, proj, x, w_a, w_b, w_o, g, layer):
    m, d = x.shape
    tm = min(MERGE_TM, m)
    gate_col = 6 * ATTN_WIDTH // d
    return pl.pallas_call(
        _merge_body,
        grid=(m // tm,),
        in_specs=[
            pl.BlockSpec((tm, ATTN_WIDTH), lambda i: (i, 0)),
            pl.BlockSpec((tm, ATTN_WIDTH), lambda i: (i, 0)),
            pl.BlockSpec((tm, d), lambda i: (i, gate_col)),
            pl.BlockSpec((tm, d), lambda i: (i, gate_col + 1)),
            pl.BlockSpec((tm, d), lambda i: (i, 0)),
            pl.BlockSpec((None, ATTN_WIDTH, d), lambda i: (layer, 0, 0), **ONCE),
            pl.BlockSpec((None, ATTN_WIDTH, d), lambda i: (layer, 0, 0), **ONCE),
            pl.BlockSpec((None, d, d), lambda i: (layer, 0, 0), **ONCE),
            pl.BlockSpec((None, 1, d), lambda i: (layer, 0, 0)),
        ],
        out_specs=pl.BlockSpec((tm, d), lambda i: (i, 0)),
        out_shape=jax.ShapeDtypeStruct((m, d), F32),
        compiler_params=_params("arbitrary"),
        name="merge",
    )(ya, yb, proj, proj, x, w_a, w_b, w_o, g)


def _trunk(x, ffn1_pre_g, ffn1_w_gu, ffn1_w_down, ffn1_post_g, mix_pre_g, w_in,
           diff_lq1, diff_lk1, diff_lq2, diff_lk2, diff_subln_g,
           w_branch_diff, w_branch_moba, w_out, mix_post_g,
           ffn2_pre_g, ffn2_w_gu, ffn2_w_down, ffn2_post_g):
    b, seq, d = x.shape
    depth = w_in.shape[0]
    assert (6 * ATTN_WIDTH) % d == 0 and seq % ATTN_TK == 0
    row = lambda p: p[:, None, :]
    bf = lambda w: w.astype(BF16)
    cos_d, sin_d = _rope_tables_t(seq, DIFF_HEAD_DIM)
    cos_m, sin_m = _rope_tables_t(seq, MOBA_HEAD_DIM)
    w1gu, w1d, w2gu, w2d = bf(ffn1_w_gu), bf(ffn1_w_down), bf(ffn2_w_gu), bf(ffn2_w_down)
    win, wa, wb, wo = bf(w_in), bf(w_branch_diff), bf(w_branch_moba), bf(w_out)
    xt = x.reshape(b * seq, d)
    for l in range(depth):
        lambda_init = 0.8 - 0.6 * math.exp(-0.3 * l)
        xt = _ffn(xt, row(ffn1_pre_g), w1gu, w1d, row(ffn1_post_g), l)
        proj = _inproj(xt, row(mix_pre_g), win, l)
        proj3 = proj.reshape(b, seq, proj.shape[1])
        ya = _diff_attention(proj3, cos_d, sin_d, row(diff_lq1), row(diff_lk1), row(diff_lq2),
                             row(diff_lk2), row(diff_subln_g), l, lambda_init)
        yb = _moba_attention(proj3, cos_m, sin_m)
        xt = _merge(ya.reshape(b * seq, ATTN_WIDTH), yb.reshape(b * seq, ATTN_WIDTH), proj, xt,
                    wa, wb, wo, row(mix_post_g), l)
        xt = _ffn(xt, row(ffn2_pre_g), w2gu, w2d, row(ffn2_post_g), l)
    return xt.reshape(b, seq, d)


def kernel(x, ffn1_pre_g, ffn1_w_gu, ffn1_w_down, ffn1_post_g, mix_pre_g, w_in, diff_lq1, diff_lk1, diff_lq2, diff_lk2, diff_subln_g, w_branch_diff, w_branch_moba, w_out, mix_post_g, ffn2_pre_g, ffn2_w_gu, ffn2_w_down, ffn2_post_g):
    return _trunk(x, ffn1_pre_g, ffn1_w_gu, ffn1_w_down, ffn1_post_g, mix_pre_g, w_in,
                  diff_lq1, diff_lk1, diff_lq2, diff_lk2, diff_subln_g,
                  w_branch_diff, w_branch_moba, w_out, mix_post_g,
                  ffn2_pre_g, ffn2_w_gu, ffn2_w_down, ffn2_post_g)
```
